```python
import math
import jax, jax.numpy as jnp
from jax import lax
import numpy as np

D_MODEL = 1024
BATCH = 8
SEQ = 4096
DEPTH = 1

D_MIX = D_MODEL
ATT_HEADS = 8
ATT_HEAD_DIM = 64
ATT_WIDTH = ATT_HEADS * ATT_HEAD_DIM
DILATED_GROUPS = ((128, 1), (512, 4), (2048, 16))
ATT_BLOCK = 128
MLSTM_HEADS = 4
MLSTM_HEAD_DIM = 128
MLSTM_WIDTH = MLSTM_HEADS * MLSTM_HEAD_DIM
MLSTM_CHUNK = 64
MLSTM_CONV = 4
D_FF = 2816
FFN_CONV = 3
EPS = 1e-6
PROJ_WIDTH = 3 * ATT_WIDTH + 4 * MLSTM_WIDTH + 2 * MLSTM_HEADS
SPLITS = (ATT_WIDTH, 2 * ATT_WIDTH, 3 * ATT_WIDTH,
          3 * ATT_WIDTH + MLSTM_WIDTH, 3 * ATT_WIDTH + 2 * MLSTM_WIDTH,
          3 * ATT_WIDTH + 3 * MLSTM_WIDTH, 3 * ATT_WIDTH + 4 * MLSTM_WIDTH,
          3 * ATT_WIDTH + 4 * MLSTM_WIDTH + MLSTM_HEADS)

kernel_name = "hymba_dilated_mlstm_convffn"


def _rmsnorm(x, w):
    xf = x.astype(jnp.float32)
    y = xf * lax.rsqrt(jnp.mean(xf * xf, axis=-1, keepdims=True) + EPS)
    return (y * w.astype(jnp.float32)).astype(x.dtype)


def _causal_dwconv(x, w, b):
    K = w.shape[0]
    S = x.shape[1]
    xp = jnp.pad(x, ((0, 0), (K - 1, 0), (0, 0)))
    y = b
    for j in range(K):
        y = y + xp[:, j:j + S, :] * w[j]
    return y


def _banded_attention(q, k, v, span):
    N, L, H, Dh = q.shape
    nb = -(-L // ATT_BLOCK)
    Lp = nb * ATT_BLOCK
    pad = ((0, 0), (0, Lp - L), (0, 0), (0, 0))

    def blocks(t):
        return jnp.pad(t.astype(jnp.float32), pad).reshape(N, nb, ATT_BLOCK, H, Dh)

    qb, kb, vb = blocks(q), blocks(k), blocks(v)

    def with_prev(t):
        prev = jnp.pad(t, ((0, 0), (1, 0), (0, 0), (0, 0), (0, 0)))[:, :-1]
        return jnp.concatenate([prev, t], axis=2)

    kk, vv = with_prev(kb), with_prev(vb)
    s = jnp.einsum('nbqhd,nbkhd->nbhqk', qb, kk) * (Dh ** -0.5)
    qi = jnp.arange(ATT_BLOCK)[:, None]
    kj = jnp.arange(2 * ATT_BLOCK)[None, :]
    dist = ATT_BLOCK + qi - kj
    key_pos = (jnp.arange(nb)[:, None, None] - 1) * ATT_BLOCK + kj[None]
    mask = (dist >= 0) & (dist <= span) & (key_pos >= 0)
    s = jnp.where(mask[None, :, None], s, -jnp.inf)
    m = jnp.max(s, axis=-1, keepdims=True)
    p = jnp.exp(s - m)
    den = jnp.sum(p, axis=-1, keepdims=True)
    o = jnp.einsum('nbhqk,nbkhd->nbqhd', p / den, vv).reshape(N, Lp, H, Dh)[:, :L]
    lse = (m + jnp.log(den))[..., 0].transpose(0, 1, 3, 2).reshape(N, Lp, H)[:, :L]
    return o, lse


def _dilated_attention(q, k, v):
    B, S, H, Dh = q.shape
    outs, lses = [], []
    for window, dil in DILATED_GROUPS:
        L = S // dil

        def to_sub(t):
            return t.reshape(B, L, dil, H, Dh).transpose(0, 2, 1, 3, 4).reshape(B * dil, L, H, Dh)

        o, lse = _banded_attention(to_sub(q), to_sub(k), to_sub(v), window // dil)
        outs.append(o.reshape(B, dil, L, H, Dh).transpose(0, 2, 1, 3, 4).reshape(B, S, H, Dh))
        lses.append(lse.reshape(B, dil, L, H).transpose(0, 2, 1, 3).reshape(B, S, H))
    wts = jax.nn.softmax(jnp.stack(lses, axis=0), axis=0)
    return jnp.sum(wts[..., None] * jnp.stack(outs, axis=0), axis=0)


def _mlstm_chunkwise(q, k, v, log_i, log_f):
    B, S, H, D = q.shape
    Lc = MLSTM_CHUNK
    nc = S // Lc

    def chunks(t):
        return t.reshape(B, nc, Lc, H, D).transpose(0, 3, 1, 2, 4)

    def chunks_g(t):
        return t.reshape(B, nc, Lc, H).transpose(0, 3, 1, 2)

    qc, kc, vc = chunks(q), chunks(k), chunks(v)
    li, lf = chunks_g(log_i), chunks_g(log_f)
    bcum = jnp.cumsum(lf, axis=-1)
    g = bcum[..., -1]
    a = g[..., None] - bcum + li

    def step(carry, xs):
        C, n, m = carry
        k_j, v_j, a_j, g_j = xs
        m_new = jnp.maximum(g_j + m, jnp.max(a_j, axis=-1))
        decay = jnp.exp(g_j + m - m_new)
        w = jnp.exp(a_j - m_new[..., None])
        C_new = decay[..., None, None] * C + jnp.einsum('bhs,bhsv,bhsk->bhvk', w, v_j, k_j)
        n_new = decay[..., None] * n + jnp.einsum('bhs,bhsk->bhk', w, k_j)
        return (C_new, n_new, m_new), (C, n, m)

    init = (jnp.zeros((B, H, D, D), jnp.float32), jnp.zeros((B, H, D), jnp.float32),
            jnp.zeros((B, H), jnp.float32))
    xs = (kc.transpose(2, 0, 1, 3, 4), vc.transpose(2, 0, 1, 3, 4),
          a.transpose(2, 0, 1, 3), g.transpose(2, 0, 1))
    _, (C_prev, n_prev, m_prev) = lax.scan(step, init, xs)
    C_prev = C_prev.transpose(1, 2, 0, 3, 4)
    n_prev = n_prev.transpose(1, 2, 0, 3)
    m_prev = m_prev.transpose(1, 2, 0)

    causal = jnp.tril(jnp.ones((Lc, Lc), dtype=bool))
    dmat = jnp.where(causal, bcum[..., :, None] - bcum[..., None, :] + li[..., None, :], -jnp.inf)
    inter_log = bcum + m_prev[..., None]
    m_t = jnp.maximum(inter_log, jnp.max(dmat, axis=-1))
    scores = jnp.einsum('bhctd,bhcsd->bhcts', qc, kc) * jnp.exp(dmat - m_t[..., None])
    inter = jnp.exp(inter_log - m_t)
    num = (jnp.einsum('bhcts,bhcsd->bhctd', scores, vc)
           + inter[..., None] * jnp.einsum('bhcvk,bhctk->bhctv', C_prev, qc))
    den = jnp.sum(scores, axis=-1) + inter * jnp.einsum('bhck,bhctk->bhct', n_prev, qc)
    h = num / jnp.maximum(jnp.abs(den), jnp.exp(-m_t))[..., None]
    return h.transpose(0, 2, 3, 1, 4).reshape(B, S, H, D)


def _hybrid_mixer(h, w_in, mlstm_conv_w, mlstm_conv_b, mlstm_i_bias, mlstm_f_bias,
                  att_out_norm_w, mlstm_out_norm_w, w_out):
    B, S, _ = h.shape
    proj = h @ w_in
    aq, ak, av, mq, mk, mv, mo, mig, mfg = jnp.split(proj, SPLITS, axis=-1)

    def att_heads(t):
        return t.reshape(B, S, ATT_HEADS, ATT_HEAD_DIM)

    att = _dilated_attention(att_heads(aq), att_heads(ak), att_heads(av))
    att = _rmsnorm(att, att_out_norm_w.reshape(ATT_HEADS, ATT_HEAD_DIM)).reshape(B, S, ATT_WIDTH)

    mqk = jax.nn.silu(_causal_dwconv(jnp.concatenate([mq, mk], axis=-1), mlstm_conv_w, mlstm_conv_b))
    mq, mk = jnp.split(mqk, 2, axis=-1)

    def ml_heads(t):
        return t.astype(jnp.float32).reshape(B, S, MLSTM_HEADS, MLSTM_HEAD_DIM)

    log_i = (mig + mlstm_i_bias).astype(jnp.float32)
    log_f = jax.nn.log_sigmoid((mfg + mlstm_f_bias).astype(jnp.float32))
    hm = _mlstm_chunkwise(ml_heads(mq), ml_heads(mk) * (MLSTM_HEAD_DIM ** -0.5), ml_heads(mv),
                          log_i, log_f)
    hm = _rmsnorm(hm, mlstm_out_norm_w.reshape(MLSTM_HEADS, MLSTM_HEAD_DIM)).reshape(B, S, MLSTM_WIDTH)
    hm = jax.nn.sigmoid(mo.astype(jnp.float32)) * hm

    y = jnp.concatenate([att, hm], axis=-1).astype(h.dtype)
    return y @ w_out


def _conv_ffn(h, w_ffn_up, ffn_conv_w, ffn_conv_b, w_ffn_down):
    u = _causal_dwconv(h @ w_ffn_up, ffn_conv_w, ffn_conv_b)
    gate, val = jnp.split(u, 2, axis=-1)
    return (jax.nn.silu(gate) * val) @ w_ffn_down


def setup_inputs(seed: int = 0) -> dict:
    key = jax.random.key(seed)
    ks = jax.random.split(key, 17)
    f32 = jnp.float32

    def nrm(k, shape, scale):
        return jax.random.normal(k, shape, f32) * scale

    def gain(k, shape):
        return 1.0 + 0.02 * jax.random.normal(k, shape, f32)

    f_bias = (jnp.linspace(3.0, 6.0, MLSTM_HEADS, dtype=f32)[None, :]
              + 0.1 * jax.random.normal(ks[5], (DEPTH, MLSTM_HEADS), f32))
    return {
        "x": jax.random.normal(ks[0], (BATCH, SEQ, D_MODEL), f32),
        "w_in": nrm(ks[1], (DEPTH, D_MODEL, PROJ_WIDTH), D_MODEL ** -0.5),
        "mlstm_conv_w": nrm(ks[2], (DEPTH, MLSTM_CONV, 2 * MLSTM_WIDTH), MLSTM_CONV ** -0.5),
        "mlstm_conv_b": nrm(ks[3], (DEPTH, 2 * MLSTM_WIDTH), 0.01),
        "mlstm_i_bias": nrm(ks[4], (DEPTH, MLSTM_HEADS), 0.1),
        "mlstm_f_bias": f_bias,
        "att_out_norm_w": gain(ks[6], (DEPTH, ATT_WIDTH)),
        "mlstm_out_norm_w": gain(ks[7], (DEPTH, MLSTM_WIDTH)),
        "w_out": nrm(ks[8], (DEPTH, D_MIX, D_MODEL), D_MIX ** -0.5),
        "mixer_norm_w": gain(ks[9], (DEPTH, D_MODEL)),
        "ffn_norm_w": gain(ks[10], (DEPTH, D_MODEL)),
        "w_ffn_up": nrm(ks[11], (DEPTH, D_MODEL, 2 * D_FF), D_MODEL ** -0.5),
        "ffn_conv_w": nrm(ks[12], (DEPTH, FFN_CONV, 2 * D_FF), FFN_CONV ** -0.5),
        "ffn_conv_b": nrm(ks[13], (DEPTH, 2 * D_FF), 0.01),
        "w_ffn_down": nrm(ks[14], (DEPTH, D_FF, D_MODEL), D_FF ** -0.5),
        "final_norm_w": gain(ks[15], (D_MODEL,)),
    }


def reference(x, w_in, mlstm_conv_w, mlstm_conv_b, mlstm_i_bias, mlstm_f_bias,
              att_out_norm_w, mlstm_out_norm_w, w_out, mixer_norm_w, ffn_norm_w,
              w_ffn_up, ffn_conv_w, ffn_conv_b, w_ffn_down, final_norm_w):
    h = x
    for layer in range(DEPTH):
        h = h + _hybrid_mixer(_rmsnorm(h, mixer_norm_w[layer]), w_in[layer],
                              mlstm_conv_w[layer], mlstm_conv_b[layer],
                              mlstm_i_bias[layer], mlstm_f_bias[layer],
                              att_out_norm_w[layer], mlstm_out_norm_w[layer], w_out[layer])
        h = h + _conv_ffn(_rmsnorm(h, ffn_norm_w[layer]), w_ffn_up[layer],
                          ffn_conv_w[layer], ffn_conv_b[layer], w_ffn_down[layer])
    return _rmsnorm(h, final_norm_w)
```

```python
import functools

import numpy as np
import jax
import jax.numpy as jnp
from jax import lax
from jax.experimental import pallas as pl
from jax.experimental.pallas import tpu as pltpu

F32 = jnp.float32
BF16 = jnp.bfloat16

EPS = 1e-6
LANES = 128
ATT_HEADS = 8
ATT_HEAD_DIM = 64
ATT_WIDTH = ATT_HEADS * ATT_HEAD_DIM
ATT_PAIRS = ATT_WIDTH // LANES
ATT_BLOCK = 128
DILATIONS = (16, 4, 1)
MAX_DIL = 16
ATT_TILE = ATT_BLOCK * MAX_DIL
MLSTM_HEADS = 4
MLSTM_HEAD_DIM = 128
MLSTM_WIDTH = MLSTM_HEADS * MLSTM_HEAD_DIM
MLSTM_CHUNK = 64
MLSTM_CONV = 4
FFN_CONV = 3
HALO = 8
NEG = -1e30
VMEM_LIMIT = 56 * 1024 * 1024

PROJ_TILE = 512
FFN_TILE = 512
FFN_CHUNK = 256
MLSTM_TILE = 1024


def _rms(x, w):
    return x * lax.rsqrt(jnp.mean(x * x, axis=-1, keepdims=True) + EPS) * w


def _proj_kernel(x_ref, nw_ref, w_ref, cw_ref, cb_ref, gb_ref,
                 aq_ref, ak_ref, av_ref, mq_ref, mk_ref, mv_ref, mo_ref, g_ref,
                 cs_ref, *, tiles_per_seq):
    tm = x_ref.shape[0]
    i = pl.program_id(0)
    xn = _rms(x_ref[...], nw_ref[...]).astype(BF16)

    def seg(c0, width):
        return jnp.dot(xn, w_ref[:, c0:c0 + width], preferred_element_type=F32)

    def put_blocks(ref, y):
        for p in range(ref.shape[0]):
            ref[p] = y[:, LANES * p:LANES * (p + 1)].astype(BF16)

    put_blocks(aq_ref, seg(0, ATT_WIDTH) * (ATT_HEAD_DIM ** -0.5))
    put_blocks(ak_ref, seg(ATT_WIDTH, ATT_WIDTH))
    put_blocks(av_ref, seg(2 * ATT_WIDTH, ATT_WIDTH))

    base = 3 * ATT_WIDTH

    @pl.when(i % tiles_per_seq == 0)
    def _():
        cs_ref[0:HALO, :] = jnp.zeros((HALO, cs_ref.shape[1]), F32)

    cs_ref[HALO:HALO + tm, :] = seg(base, 2 * MLSTM_WIDTH)
    y = cb_ref[...]
    for j in range(MLSTM_CONV):
        off = HALO - (MLSTM_CONV - 1) + j
        y = y + cs_ref[off:off + tm, :] * cw_ref[j:j + 1, :]
    y = y * jax.nn.sigmoid(y)
    put_blocks(mq_ref, y[:, :MLSTM_WIDTH])
    put_blocks(mk_ref, y[:, MLSTM_WIDTH:] * (MLSTM_HEAD_DIM ** -0.5))
    cs_ref[0:HALO, :] = cs_ref[tm:tm + HALO, :]

    put_blocks(mv_ref, seg(base + 2 * MLSTM_WIDTH, MLSTM_WIDTH))
    put_blocks(mo_ref, seg(base + 3 * MLSTM_WIDTH, MLSTM_WIDTH))

    z = seg(base + 4 * MLSTM_WIDTH, LANES) + gb_ref[...]
    lane = lax.broadcasted_iota(jnp.int32, z.shape, 1)
    log_sig = jnp.minimum(z, 0.0) - jnp.log1p(jnp.exp(-jnp.abs(z)))
    g_ref[...] = jnp.where(lane < MLSTM_HEADS, z, log_sig)


def _input_projection(x2d, norm_w, w_pad, conv_w, conv_b, gate_b, seq_len):
    T, D = x2d.shape
    tm = PROJ_TILE
    assert T % tm == 0 and seq_len % tm == 0
    row = lambda i: (i, 0)
    const2 = lambda i: (0, 0)
    blk3 = lambda i: (0, i, 0)
    heads_shape = jax.ShapeDtypeStruct((ATT_PAIRS, T, LANES), BF16)
    heads_spec = pl.BlockSpec((ATT_PAIRS, tm, LANES), blk3)
    return pl.pallas_call(
        functools.partial(_proj_kernel, tiles_per_seq=seq_len // tm),
        grid=(T // tm,),
        in_specs=[
            pl.BlockSpec((tm, D), row),
            pl.BlockSpec((1, D), const2),
            pl.BlockSpec(w_pad.shape, const2),
            pl.BlockSpec(conv_w.shape, const2),
            pl.BlockSpec(conv_b.shape, const2),
            pl.BlockSpec(gate_b.shape, const2),
        ],
        out_specs=[heads_spec] * 7 + [pl.BlockSpec((tm, LANES), row)],
        out_shape=[heads_shape] * 7 + [jax.ShapeDtypeStruct((T, LANES), F32)],
        scratch_shapes=[pltpu.VMEM((HALO + tm, 2 * MLSTM_WIDTH), F32)],
        compiler_params=pltpu.CompilerParams(
            dimension_semantics=("arbitrary",), vmem_limit_bytes=VMEM_LIMIT),
        name="input_projection",
    )(x2d, norm_w, w_pad, conv_w, conv_b, gate_b)


def _attention_biases():
    n = np.arange(ATT_BLOCK)
    s = np.arange(2 * ATT_BLOCK)
    out = np.zeros((3, 2, 2 * ATT_BLOCK, 2 * ATT_BLOCK), np.float32)
    pq16, pk16 = n, s
    pq4 = 4 * (n % 32) + n // 32
    pk4 = 4 * (s % 64) + s // 64
    perm1 = 16 * (n % 8) + n // 8
    pq1 = perm1
    pk1 = perm1[s % ATT_BLOCK] + ATT_BLOCK * (s // ATT_BLOCK)
    shifts = (ATT_BLOCK, 4 * 32, ATT_BLOCK)
    for g, (pq, pk, shift) in enumerate(((pq16, pk16, shifts[0]), (pq4, pk4, shifts[1]), (pq1, pk1, shifts[2]))):
        for variant, sh in enumerate((shift, 0)):
            dist = pq[:, None] - (pk[None, :] - sh)
            ok = (dist >= 0) & (dist <= ATT_BLOCK)
            bias = np.where(ok, 0.0, NEG).astype(np.float32)
            out[g, variant] = np.concatenate([bias, bias], axis=0)
    return out


def _attn_unit(q, kk, vv, bias, half0):
    zero = jnp.zeros_like(q)
    q2 = jnp.concatenate([jnp.where(half0, q, zero), jnp.where(half0, zero, q)], axis=0)
    s = lax.dot_general(q2, kk, (((1,), (1,)), ((), ())), preferred_element_type=F32) + bias
    m = jnp.max(s, axis=-1, keepdims=True)
    p = jnp.exp(s - m).astype(BF16)
    vext = jnp.concatenate([vv, jnp.ones_like(vv)], axis=1)
    res = jnp.dot(p, vext, preferred_element_type=F32)
    top, bot = res[:ATT_BLOCK], res[ATT_BLOCK:]
    o = jnp.where(half0, top[:, :LANES], bot[:, :LANES])
    den = jnp.where(half0, top[:, LANES:], bot[:, LANES:])
    mb = jnp.where(half0, m[:ATT_BLOCK], m[ATT_BLOCK:])
    return o, mb, den


def _attention_kernel(q_ref, k_ref, v_ref, bias_ref, nw_ref, o_ref, qf_ref, kf_ref, vf_ref, st_ref):
    jt = pl.program_id(2)
    half0 = lax.broadcasted_iota(jnp.int32, (ATT_BLOCK, LANES), 1) < ATT_HEAD_DIM
    row0 = pl.multiple_of(jt * ATT_BLOCK, ATT_BLOCK)

    def lanes(r):
        return slice(LANES * r, LANES * (r + 1))

    @pl.when(jt == 0)
    def _():
        qf_ref[...] = q_ref[...].astype(F32)
        kf_ref[...] = k_ref[...].astype(F32)
        vf_ref[...] = v_ref[...].astype(F32)

    def store_stats(g, r, rows, stats):
        for st, val in enumerate(stats):
            st_ref[g, st, r, rows, :] = val

    first16 = jt == 0
    k0 = pl.multiple_of(jnp.maximum(row0 - ATT_BLOCK, 0), ATT_BLOCK)
    bias16 = bias_ref[0, jnp.where(first16, 1, 0)]
    for r in range(MAX_DIL):
        q = q_ref[pl.ds(row0, ATT_BLOCK), lanes(r)]
        kk = k_ref[pl.ds(k0, 2 * ATT_BLOCK), lanes(r)]
        vv = v_ref[pl.ds(k0, 2 * ATT_BLOCK), lanes(r)]
        store_stats(0, r, slice(None), _attn_unit(q, kk, vv, bias16, half0))

    def body4(bb, carry):
        gb = jt * 4 + bb
        q0 = pl.multiple_of(gb * 32, 32)
        k0 = pl.multiple_of(jnp.maximum(gb * 32 - 32, 0), 32)
        bias = bias_ref[1, jnp.where(gb == 0, 1, 0)]
        o0 = pl.multiple_of(bb * 32, 32)
        for r0 in range(4):
            q = jnp.concatenate([q_ref[pl.ds(q0, 32), lanes(4 * r1 + r0)] for r1 in range(4)], axis=0)
            kk = jnp.concatenate([k_ref[pl.ds(k0, 64), lanes(4 * r1 + r0)] for r1 in range(4)], axis=0)
            vv = jnp.concatenate([v_ref[pl.ds(k0, 64), lanes(4 * r1 + r0)] for r1 in range(4)], axis=0)
            stats = _attn_unit(q, kk, vv, bias, half0)
            for r1 in range(4):
                store_stats(1, 4 * r1 + r0, pl.ds(o0, 32), [s[32 * r1:32 * (r1 + 1)] for s in stats])
        return carry

    lax.fori_loop(0, 4, body4, 0)

    def body1(bl, carry):
        b = jt * MAX_DIL + bl
        q0 = pl.multiple_of(b * 8, 8)
        k0 = pl.multiple_of(jnp.maximum(b * 8 - 8, 0), 8)
        bias = bias_ref[2, jnp.where(b == 0, 1, 0)]
        o0 = pl.multiple_of(bl * 8, 8)

        def gather(ref, start):
            return jnp.concatenate([ref[pl.ds(start, 8), lanes(r)] for r in range(MAX_DIL)], axis=0)

        q = gather(qf_ref, q0).astype(BF16)
        kk = jnp.concatenate([gather(kf_ref, k0), gather(kf_ref, k0 + 8)], axis=0).astype(BF16)
        vv = jnp.concatenate([gather(vf_ref, k0), gather(vf_ref, k0 + 8)], axis=0).astype(BF16)
        stats = _attn_unit(q, kk, vv, bias, half0)
        for r in range(MAX_DIL):
            store_stats(2, r, pl.ds(o0, 8), [s[8 * r:8 * (r + 1)] for s in stats])
        return carry

    lax.fori_loop(0, MAX_DIL, body1, 0)

    nw = nw_ref[...]
    for r in range(MAX_DIL):
        ms = [st_ref[g, 1, r] for g in range(3)]
        mx = jnp.maximum(jnp.maximum(ms[0], ms[1]), ms[2])
        fs = [jnp.exp(m - mx) for m in ms]
        num = fs[0] * st_ref[0, 0, r] + fs[1] * st_ref[1, 0, r] + fs[2] * st_ref[2, 0, r]
        den = fs[0] * st_ref[0, 2, r] + fs[1] * st_ref[1, 2, r] + fs[2] * st_ref[2, 2, r]
        att = num / den
        a2 = att * att
        s0 = jnp.sum(jnp.where(half0, a2, 0.0), axis=-1, keepdims=True)
        s1 = jnp.sum(jnp.where(half0, 0.0, a2), axis=-1, keepdims=True)
        msq = jnp.where(half0, s0, s1) * (1.0 / ATT_HEAD_DIM)
        o_ref[pl.ds(row0, ATT_BLOCK), lanes(r)] = (att * lax.rsqrt(msq + EPS) * nw).astype(BF16)


def _dilated_attention(aq, ak, av, norm_w, batch, seq_len):
    assert seq_len % ATT_TILE == 0 and seq_len >= 2 * ATT_TILE
    rows = seq_len // MAX_DIL
    nt = seq_len // ATT_TILE
    view = lambda t: t.reshape(ATT_PAIRS, batch, rows, MAX_DIL * LANES)
    bias = jnp.asarray(_attention_biases())
    seq_spec = pl.BlockSpec((None, None, rows, MAX_DIL * LANES), lambda b, p, j: (p, b, 0, 0))
    out = pl.pallas_call(
        _attention_kernel,
        grid=(batch, ATT_PAIRS, nt),
        in_specs=[seq_spec, seq_spec, seq_spec,
                  pl.BlockSpec(bias.shape, lambda b, p, j: (0, 0, 0, 0)),
                  pl.BlockSpec((None, 1, LANES), lambda b, p, j: (p, 0, 0))],
        out_specs=seq_spec,
        out_shape=jax.ShapeDtypeStruct((ATT_PAIRS, batch, rows, MAX_DIL * LANES), BF16),
        scratch_shapes=[pltpu.VMEM((rows, MAX_DIL * LANES), F32)] * 3
        + [pltpu.VMEM((3, 3, MAX_DIL, ATT_BLOCK, LANES), F32)],
        compiler_params=pltpu.CompilerParams(
            dimension_semantics=("arbitrary", "arbitrary", "arbitrary"), vmem_limit_bytes=VMEM_LIMIT),
        name="dilated_attention",
    )(view(aq), view(ak), view(av), bias, norm_w.reshape(ATT_PAIRS, 1, LANES))
    return out.reshape(ATT_PAIRS, batch * seq_len, LANES)


def _mlstm_kernel(q_ref, k_ref, v_ref, o_ref, g_ref, nw_ref, h_ref, c_ref, n_ref, m_ref):
    Lc, D, H = MLSTM_CHUNK, MLSTM_HEAD_DIM, MLSTM_HEADS
    ts = q_ref.shape[1]

    @pl.when(pl.program_id(1) == 0)
    def _():
        c_ref[...] = jnp.zeros(c_ref.shape, F32)
        n_ref[...] = jnp.zeros(n_ref.shape, F32)
        m_ref[...] = jnp.zeros(m_ref.shape, F32)

    ti = lax.broadcasted_iota(jnp.int32, (Lc, Lc), 0)
    si = lax.broadcasted_iota(jnp.int32, (Lc, Lc), 1)
    eye = ti == si
    causal = si <= ti

    def to_row(col):
        return jnp.sum(jnp.where(eye, col, 0.0), axis=0, keepdims=True)

    def to_col(row):
        return jnp.sum(jnp.where(eye, row, 0.0), axis=1, keepdims=True)

    def chunk(c, carry):
        r0 = pl.multiple_of(c * Lc, Lc)
        gates = g_ref[pl.ds(r0, Lc), :]
        for h in range(H):
            q = q_ref[h, pl.ds(r0, Lc), :]
            k = k_ref[h, pl.ds(r0, Lc), :]
            v = v_ref[h, pl.ds(r0, Lc), :]
            li_col = gates[:, h:h + 1]
            lf_col = gates[:, H + h:H + h + 1]
            li_row = to_row(li_col)
            bcum_row = jnp.sum(jnp.where(ti <= si, lf_col, 0.0), axis=0, keepdims=True)
            bcum_col = to_col(bcum_row)
            g = bcum_row[:, Lc - 1:Lc]
            m_prev = m_ref[h, 0:1, 0:1]

            dmat = jnp.where(causal, bcum_col - bcum_row + li_row, NEG)
            inter_log = bcum_col + m_prev
            m_t = jnp.maximum(inter_log, jnp.max(dmat, axis=1, keepdims=True))
            qk = lax.dot_general(q, k, (((1,), (1,)), ((), ())), preferred_element_type=F32)
            scores = qk * jnp.exp(dmat - m_t)
            inter = jnp.exp(inter_log - m_t)
            qc = jnp.dot(q, c_ref[h].astype(BF16), preferred_element_type=F32)
            num = jnp.dot(scores.astype(BF16), v, preferred_element_type=F32) + inter * qc
            qf = q.astype(F32)
            qn = jnp.sum(qf * n_ref[h, 0:1, :], axis=1, keepdims=True)
            den = jnp.sum(scores, axis=1, keepdims=True) + inter * qn
            hh = num / jnp.maximum(jnp.abs(den), jnp.exp(-m_t))
            hh = _rms(hh, nw_ref[h])
            gate = jax.nn.sigmoid(o_ref[h, pl.ds(r0, Lc), :].astype(F32))
            h_ref[h, pl.ds(r0, Lc), :] = (gate * hh).astype(BF16)

            a_col = g - bcum_col + li_col
            m_new = jnp.maximum(g + m_prev, jnp.max(a_col, axis=0, keepdims=True))
            decay = jnp.exp(g + m_prev - m_new)
            kw = k.astype(F32) * jnp.exp(a_col - m_new)
            dc = lax.dot_general(kw.astype(BF16), v, (((0,), (0,)), ((), ())), preferred_element_type=F32)
            c_ref[h] = decay * c_ref[h] + dc
            n_ref[h, 0:1, :] = decay * n_ref[h, 0:1, :] + jnp.sum(kw, axis=0, keepdims=True)
            m_ref[h] = jnp.broadcast_to(m_new, m_ref.shape[1:])
        return carry

    lax.fori_loop(0, ts // Lc, chunk, 0)


def _mlstm(mq, mk, mv, mo, gates, norm_w, batch, seq_len):
    H, T, D = mq.shape
    ts = MLSTM_TILE
    assert seq_len % ts == 0
    nt = seq_len // ts
    hspec = pl.BlockSpec((H, ts, D), lambda b, j: (0, b * nt + j, 0))
    return pl.pallas_call(
        _mlstm_kernel,
        grid=(batch, nt),
        in_specs=[hspec, hspec, hspec, hspec,
                  pl.BlockSpec((ts, LANES), lambda b, j: (b * nt + j, 0)),
                  pl.BlockSpec((H, 1, D), lambda b, j: (0, 0, 0))],
        out_specs=hspec,
        out_shape=jax.ShapeDtypeStruct((H, T, D), BF16),
        scratch_shapes=[pltpu.VMEM((H, D, D), F32), pltpu.VMEM((H, HALO, D), F32),
                        pltpu.VMEM((H, HALO, LANES), F32)],
        compiler_params=pltpu.CompilerParams(
            dimension_semantics=("arbitrary", "arbitrary"), vmem_limit_bytes=VMEM_LIMIT),
        name="mlstm",
    )(mq, mk, mv, mo, gates, norm_w.reshape(H, 1, D))


def _ffn_kernel(x_ref, att_ref, hm_ref, wout_ref, fnw_ref, wup_ref, cw_ref, cb_ref, wdown_ref, finw_ref,
                o_ref, us_ref, *, tiles_per_seq, final_norm):
    tm = x_ref.shape[0]
    nf, _, fc2 = wup_ref.shape
    fc = fc2 // 2
    i = pl.program_id(0)

    h1 = x_ref[...]
    for p in range(ATT_PAIRS):
        h1 = h1 + jnp.dot(att_ref[p], wout_ref[LANES * p:LANES * (p + 1), :], preferred_element_type=F32)
    for h in range(MLSTM_HEADS):
        r0 = ATT_WIDTH + MLSTM_HEAD_DIM * h
        h1 = h1 + jnp.dot(hm_ref[h], wout_ref[r0:r0 + MLSTM_HEAD_DIM, :], preferred_element_type=F32)
    xn = _rms(h1, fnw_ref[...]).astype(BF16)

    @pl.when(i % tiles_per_seq == 0)
    def _():
        us_ref[:, 0:HALO, :] = jnp.zeros((nf, HALO, fc2), F32)

    o_ref[...] = h1
    for c in range(nf):
        us_ref[c, HALO:HALO + tm, :] = jnp.dot(xn, wup_ref[c], preferred_element_type=F32)
        u = cb_ref[c]
        for j in range(FFN_CONV):
            off = HALO - (FFN_CONV - 1) + j
            u = u + us_ref[c, off:off + tm, :] * cw_ref[c, j:j + 1, :]
        us_ref[c, 0:HALO, :] = us_ref[c, tm:tm + HALO, :]
        gate, val = u[:, :fc], u[:, fc:]
        act = (gate * jax.nn.sigmoid(gate) * val).astype(BF16)
        o_ref[...] += jnp.dot(act, wdown_ref[c], preferred_element_type=F32)
    if final_norm:
        o_ref[...] = _rms(o_ref[...], finw_ref[...])


def _out_ffn(x2d, att, hm, w_out, ffn_norm_w, w_up, conv_w, conv_b, w_down, final_w, seq_len, final_norm):
    T, D = x2d.shape
    tm = FFN_TILE
    assert T % tm == 0 and seq_len % tm == 0
    nf, _, fc2 = w_up.shape
    row = lambda i: (i, 0)
    const2 = lambda i: (0, 0)
    const3 = lambda i: (0, 0, 0)
    blk3 = lambda i: (0, i, 0)
    once = dict(pipeline_mode=pl.Buffered(1))
    return pl.pallas_call(
        functools.partial(_ffn_kernel, tiles_per_seq=seq_len // tm, final_norm=final_norm),
        grid=(T // tm,),
        in_specs=[
            pl.BlockSpec((tm, D), row),
            pl.BlockSpec((ATT_PAIRS, tm, LANES), blk3),
            pl.BlockSpec((MLSTM_HEADS, tm, MLSTM_HEAD_DIM), blk3),
            pl.BlockSpec(w_out.shape, const2, **once),
            pl.BlockSpec((1, D), const2),
            pl.BlockSpec(w_up.shape, const3, **once),
            pl.BlockSpec(conv_w.shape, const3),
            pl.BlockSpec(conv_b.shape, const3),
            pl.BlockSpec(w_down.shape, const3, **once),
            pl.BlockSpec((1, D), const2),
        ],
        out_specs=pl.BlockSpec((tm, D), row),
        out_shape=jax.ShapeDtypeStruct((T, D), F32),
        scratch_shapes=[pltpu.VMEM((nf, HALO + tm, fc2), F32)],
        compiler_params=pltpu.CompilerParams(
            dimension_semantics=("arbitrary",), vmem_limit_bytes=VMEM_LIMIT),
        name="out_proj_conv_ffn",
    )(x2d, att, hm, w_out, ffn_norm_w, w_up, conv_w, conv_b, w_down, final_w)


def _prep_ffn_weights(w_up, conv_w, conv_b, w_down):
    d_model, two_f = w_up.shape
    f = two_f // 2
    fc = FFN_CHUNK
    assert f % fc == 0
    nf = f // fc

    def regroup(t):
        g = t[..., :f].reshape(t.shape[:-1] + (nf, fc))
        v = t[..., f:].reshape(t.shape[:-1] + (nf, fc))
        return jnp.moveaxis(jnp.concatenate([g, v], axis=-1), -2, 0)

    return (regroup(w_up).astype(BF16), regroup(conv_w), regroup(conv_b[None, :]),
            w_down.reshape(nf, fc, d_model).astype(BF16))


def kernel(x, w_in, mlstm_conv_w, mlstm_conv_b, mlstm_i_bias, mlstm_f_bias, att_out_norm_w, mlstm_out_norm_w, w_out, mixer_norm_w, ffn_norm_w, w_ffn_up, ffn_conv_w, ffn_conv_b, w_ffn_down, final_norm_w):
    batch, seq_len, d_model = x.shape
    depth = w_in.shape[0]
    h = x.reshape(batch * seq_len, d_model)
    for layer in range(depth):
        w = w_in[layer]
        n_gate = 2 * MLSTM_HEADS
        w_pad = jnp.pad(w, ((0, 0), (0, LANES - n_gate))).astype(BF16)
        gate_b = jnp.pad(jnp.concatenate([mlstm_i_bias[layer], mlstm_f_bias[layer]]), (0, LANES - n_gate))[None, :]
        aq, ak, av, mq, mk, mv, mo, gates = _input_projection(
            h, mixer_norm_w[layer][None, :], w_pad, mlstm_conv_w[layer], mlstm_conv_b[layer][None, :],
            gate_b, seq_len)
        att = _dilated_attention(aq, ak, av, att_out_norm_w[layer], batch, seq_len)
        hm = _mlstm(mq, mk, mv, mo, gates, mlstm_out_norm_w[layer], batch, seq_len)
        w_up, cw, cb, w_down = _prep_ffn_weights(w_ffn_up[layer], ffn_conv_w[layer], ffn_conv_b[layer],
                                                 w_ffn_down[layer])
        h = _out_ffn(h, att, hm, w_out[layer].astype(BF16), ffn_norm_w[layer][None, :], w_up, cw, cb, w_down,
                     final_norm_w[None, :], seq_len, final_norm=(layer == depth - 1))
    return h.reshape(batch, seq_len, d_model)
```

```python
import functools

import numpy as np
import jax
import jax.numpy as jnp
from jax import lax
from jax.experimental import pallas as pl
from jax.experimental.pallas import tpu as pltpu

F32 = jnp.float32
BF16 = jnp.bfloat16

EPS = 1e-6
LANES = 128
ATT_HEADS = 8
ATT_HEAD_DIM = 64
ATT_WIDTH = ATT_HEADS * ATT_HEAD_DIM
ATT_PAIRS = ATT_WIDTH // LANES
ATT_BLOCK = 128
DILATIONS = (16, 4, 1)
MAX_DIL = 16
ATT_TILE = ATT_BLOCK * MAX_DIL
MLSTM_HEADS = 4
MLSTM_HEAD_DIM = 128
MLSTM_WIDTH = MLSTM_HEADS * MLSTM_HEAD_DIM
MLSTM_CHUNK = 64
MLSTM_CONV = 4
FFN_CONV = 3
HALO = 8
NEG = -1e30
LOG2E = 1.4426950408889634
VMEM_LIMIT = 56 * 1024 * 1024

PROJ_TILE = 512
FFN_TILE = 512
FFN_CHUNK = 256
MLSTM_TILE = 1024
MLSTM_BATCHES = 2


def _rms(x, w):
    return x * lax.rsqrt(jnp.mean(x * x, axis=-1, keepdims=True) + EPS) * w


def _proj_kernel(x_ref, nw_ref, w_ref, cw_ref, cb_ref, gb_ref,
                 aq_ref, ak_ref, av_ref, mq_ref, mk_ref, mv_ref, mo_ref, g_ref,
                 cs_ref, *, tiles_per_seq):
    tm = x_ref.shape[0]
    i = pl.program_id(0)
    xn = _rms(x_ref[...], nw_ref[...]).astype(BF16)

    def seg(c0, width):
        return jnp.dot(xn, w_ref[:, c0:c0 + width], preferred_element_type=F32)

    def put_blocks(ref, y):
        for p in range(ref.shape[0]):
            ref[p] = y[:, LANES * p:LANES * (p + 1)].astype(BF16)

    put_blocks(aq_ref, seg(0, ATT_WIDTH) * (ATT_HEAD_DIM ** -0.5 * LOG2E))
    put_blocks(ak_ref, seg(ATT_WIDTH, ATT_WIDTH))
    put_blocks(av_ref, seg(2 * ATT_WIDTH, ATT_WIDTH))

    base = 3 * ATT_WIDTH

    @pl.when(i % tiles_per_seq == 0)
    def _():
        cs_ref[0:HALO, :] = jnp.zeros((HALO, cs_ref.shape[1]), F32)

    cs_ref[HALO:HALO + tm, :] = seg(base, 2 * MLSTM_WIDTH)
    y = cb_ref[...]
    for j in range(MLSTM_CONV):
        off = HALO - (MLSTM_CONV - 1) + j
        y = y + cs_ref[off:off + tm, :] * cw_ref[j:j + 1, :]
    y = y * jax.nn.sigmoid(y)
    put_blocks(mq_ref, y[:, :MLSTM_WIDTH])
    put_blocks(mk_ref, y[:, MLSTM_WIDTH:] * (MLSTM_HEAD_DIM ** -0.5))
    cs_ref[0:HALO, :] = cs_ref[tm:tm + HALO, :]

    put_blocks(mv_ref, seg(base + 2 * MLSTM_WIDTH, MLSTM_WIDTH))
    put_blocks(mo_ref, seg(base + 3 * MLSTM_WIDTH, MLSTM_WIDTH))

    z = seg(base + 4 * MLSTM_WIDTH, LANES) + gb_ref[...]
    g_ref[:, 0:LANES] = z
    log_sig = jnp.minimum(z, 0.0) - jnp.log1p(jnp.exp(-jnp.abs(z)))
    g_ref[:, LANES:2 * LANES] = pltpu.roll(log_sig, LANES - MLSTM_HEADS, axis=1)


def _input_projection(x2d, norm_w, w_pad, conv_w, conv_b, gate_b, seq_len):
    T, D = x2d.shape
    tm = PROJ_TILE
    assert T % tm == 0 and seq_len % tm == 0
    row = lambda i: (i, 0)
    const2 = lambda i: (0, 0)
    blk3 = lambda i: (0, i, 0)
    heads_shape = jax.ShapeDtypeStruct((ATT_PAIRS, T, LANES), BF16)
    heads_spec = pl.BlockSpec((ATT_PAIRS, tm, LANES), blk3)
    return pl.pallas_call(
        functools.partial(_proj_kernel, tiles_per_seq=seq_len // tm),
        grid=(T // tm,),
        in_specs=[
            pl.BlockSpec((tm, D), row),
            pl.BlockSpec((1, D), const2),
            pl.BlockSpec(w_pad.shape, const2),
            pl.BlockSpec(conv_w.shape, const2),
            pl.BlockSpec(conv_b.shape, const2),
            pl.BlockSpec(gate_b.shape, const2),
        ],
        out_specs=[heads_spec] * 7 + [pl.BlockSpec((tm, 2 * LANES), row)],
        out_shape=[heads_shape] * 7 + [jax.ShapeDtypeStruct((T, 2 * LANES), F32)],
        scratch_shapes=[pltpu.VMEM((HALO + tm, 2 * MLSTM_WIDTH), F32)],
        compiler_params=pltpu.CompilerParams(
            dimension_semantics=("arbitrary",), vmem_limit_bytes=VMEM_LIMIT),
        name="input_projection",
    )(x2d, norm_w, w_pad, conv_w, conv_b, gate_b)


def _attention_biases():
    n = np.arange(ATT_BLOCK)
    s = np.arange(2 * ATT_BLOCK)
    out = np.zeros((3, 2, 2 * ATT_BLOCK, 2 * ATT_BLOCK), np.float32)
    pq16, pk16 = n, s
    pq4 = 4 * (n % 32) + n // 32
    pk4 = 4 * (s % 64) + s // 64
    perm1 = 16 * (n % 8) + n // 8
    pq1 = perm1
    pk1 = perm1[s % ATT_BLOCK] + ATT_BLOCK * (s // ATT_BLOCK)
    shifts = (ATT_BLOCK, 4 * 32, ATT_BLOCK)
    for g, (pq, pk, shift) in enumerate(((pq16, pk16, shifts[0]), (pq4, pk4, shifts[1]), (pq1, pk1, shifts[2]))):
        for variant, sh in enumerate((shift, 0)):
            dist = pq[:, None] - (pk[None, :] - sh)
            ok = (dist >= 0) & (dist <= ATT_BLOCK)
            bias = np.where(ok, 0.0, NEG).astype(np.float32)
            out[g, variant] = np.concatenate([bias, bias], axis=0)
    return out


def _attn_unit(q, kk, vv, bias, half0):
    zero = jnp.zeros_like(q)
    q2 = jnp.concatenate([jnp.where(half0, q, zero), jnp.where(half0, zero, q)], axis=0)
    s = lax.dot_general(q2, kk, (((1,), (1,)), ((), ())), preferred_element_type=F32) + bias
    m = jnp.max(s, axis=-1, keepdims=True)
    p = jnp.exp2(s - m).astype(BF16)
    vext = jnp.concatenate([vv, jnp.ones_like(vv)], axis=1)
    res = jnp.dot(p, vext, preferred_element_type=F32)
    top, bot = res[:ATT_BLOCK], res[ATT_BLOCK:]
    o = jnp.where(half0, top[:, :LANES], bot[:, :LANES])
    den = jnp.where(half0, top[:, LANES:], bot[:, LANES:])
    mb = jnp.where(half0, m[:ATT_BLOCK], m[ATT_BLOCK:])
    return o, mb, den


def _attention_kernel(q_ref, k_ref, v_ref, bias_ref, nw_ref, o_ref, qf_ref, kf_ref, vf_ref, st_ref):
    jt = pl.program_id(2)
    half0 = lax.broadcasted_iota(jnp.int32, (ATT_BLOCK, LANES), 1) < ATT_HEAD_DIM
    row0 = pl.multiple_of(jt * ATT_BLOCK, ATT_BLOCK)

    def lanes(r):
        return slice(LANES * r, LANES * (r + 1))

    @pl.when(jt == 0)
    def _():
        qf_ref[...] = q_ref[...].astype(F32)
        kf_ref[...] = k_ref[...].astype(F32)
        vf_ref[...] = v_ref[...].astype(F32)

    def store_stats(g, r, rows, stats):
        for st, val in enumerate(stats):
            st_ref[g, st, r, rows, :] = val

    first16 = jt == 0
    k0 = pl.multiple_of(jnp.maximum(row0 - ATT_BLOCK, 0), ATT_BLOCK)
    bias16 = bias_ref[0, jnp.where(first16, 1, 0)]
    for r in range(MAX_DIL):
        q = q_ref[pl.ds(row0, ATT_BLOCK), lanes(r)]
        kk = k_ref[pl.ds(k0, 2 * ATT_BLOCK), lanes(r)]
        vv = v_ref[pl.ds(k0, 2 * ATT_BLOCK), lanes(r)]
        store_stats(0, r, slice(None), _attn_unit(q, kk, vv, bias16, half0))

    def body4(bb, carry):
        gb = jt * 4 + bb
        q0 = pl.multiple_of(gb * 32, 32)
        k0 = pl.multiple_of(jnp.maximum(gb * 32 - 32, 0), 32)
        bias = bias_ref[1, jnp.where(gb == 0, 1, 0)]
        o0 = pl.multiple_of(bb * 32, 32)
        for r0 in range(4):
            q = jnp.concatenate([q_ref[pl.ds(q0, 32), lanes(4 * r1 + r0)] for r1 in range(4)], axis=0)
            kk = jnp.concatenate([k_ref[pl.ds(k0, 64), lanes(4 * r1 + r0)] for r1 in range(4)], axis=0)
            vv = jnp.concatenate([v_ref[pl.ds(k0, 64), lanes(4 * r1 + r0)] for r1 in range(4)], axis=0)
            stats = _attn_unit(q, kk, vv, bias, half0)
            for r1 in range(4):
                store_stats(1, 4 * r1 + r0, pl.ds(o0, 32), [s[32 * r1:32 * (r1 + 1)] for s in stats])
        return carry

    lax.fori_loop(0, 4, body4, 0)

    def body1(bl, carry):
        b = jt * MAX_DIL + bl
        q0 = pl.multiple_of(b * 8, 8)
        k0 = pl.multiple_of(jnp.maximum(b * 8 - 8, 0), 8)
        bias = bias_ref[2, jnp.where(b == 0, 1, 0)]
        o0 = pl.multiple_of(bl * 8, 8)

        def gather(ref, start):
            return jnp.concatenate([ref[pl.ds(start, 8), lanes(r)] for r in range(MAX_DIL)], axis=0)

        q = gather(qf_ref, q0).astype(BF16)
        kk = jnp.concatenate([gather(kf_ref, k0), gather(kf_ref, k0 + 8)], axis=0).astype(BF16)
        vv = jnp.concatenate([gather(vf_ref, k0), gather(vf_ref, k0 + 8)], axis=0).astype(BF16)
        stats = _attn_unit(q, kk, vv, bias, half0)
        for r in range(MAX_DIL):
            store_stats(2, r, pl.ds(o0, 8), [s[8 * r:8 * (r + 1)] for s in stats])
        return carry

    lax.fori_loop(0, MAX_DIL, body1, 0, unroll=4)

    nw = nw_ref[...]
    for r in range(MAX_DIL):
        ms = [st_ref[g, 1, r] for g in range(3)]
        mx = jnp.maximum(jnp.maximum(ms[0], ms[1]), ms[2])
        fs = [jnp.exp2(m - mx) for m in ms]
        num = fs[0] * st_ref[0, 0, r] + fs[1] * st_ref[1, 0, r] + fs[2] * st_ref[2, 0, r]
        den = fs[0] * st_ref[0, 2, r] + fs[1] * st_ref[1, 2, r] + fs[2] * st_ref[2, 2, r]
        att = num / den
        a2 = att * att
        s0 = jnp.sum(jnp.where(half0, a2, 0.0), axis=-1, keepdims=True)
        s1 = jnp.sum(jnp.where(half0, 0.0, a2), axis=-1, keepdims=True)
        msq = jnp.where(half0, s0, s1) * (1.0 / ATT_HEAD_DIM)
        o_ref[pl.ds(row0, ATT_BLOCK), lanes(r)] = (att * lax.rsqrt(msq + EPS) * nw).astype(BF16)


def _dilated_attention(aq, ak, av, norm_w, batch, seq_len):
    assert seq_len % ATT_TILE == 0 and seq_len >= 2 * ATT_TILE
    rows = seq_len // MAX_DIL
    nt = seq_len // ATT_TILE
    view = lambda t: t.reshape(ATT_PAIRS, batch, rows, MAX_DIL * LANES)
    bias = jnp.asarray(_attention_biases())
    seq_spec = pl.BlockSpec((None, None, rows, MAX_DIL * LANES), lambda b, p, j: (p, b, 0, 0))
    out = pl.pallas_call(
        _attention_kernel,
        grid=(batch, ATT_PAIRS, nt),
        in_specs=[seq_spec, seq_spec, seq_spec,
                  pl.BlockSpec(bias.shape, lambda b, p, j: (0, 0, 0, 0)),
                  pl.BlockSpec((None, 1, LANES), lambda b, p, j: (p, 0, 0))],
        out_specs=seq_spec,
        out_shape=jax.ShapeDtypeStruct((ATT_PAIRS, batch, rows, MAX_DIL * LANES), BF16),
        scratch_shapes=[pltpu.VMEM((rows, MAX_DIL * LANES), F32)] * 3
        + [pltpu.VMEM((3, 3, MAX_DIL, ATT_BLOCK, LANES), F32)],
        compiler_params=pltpu.CompilerParams(
            dimension_semantics=("arbitrary", "arbitrary", "arbitrary"), vmem_limit_bytes=VMEM_LIMIT),
        name="dilated_attention",
    )(view(aq), view(ak), view(av), bias, norm_w.reshape(ATT_PAIRS, 1, LANES))
    return out.reshape(ATT_PAIRS, batch * seq_len, LANES)


def _scan_sublanes(x, op, fill):
    n = x.shape[0]
    row = lax.broadcasted_iota(jnp.int32, x.shape, 0)
    sh = 1
    while sh < n:
        x = op(x, jnp.where(row >= sh, pltpu.roll(x, sh, axis=0), fill))
        sh *= 2
    return x


def _mlstm_kernel(q_ref, k_ref, v_ref, o_ref, g_ref, nw_ref, h_ref, c_ref, n_ref, m_ref):
    Lc, D, H = MLSTM_CHUNK, MLSTM_HEAD_DIM, MLSTM_HEADS
    nb, ts = q_ref.shape[1], q_ref.shape[2]

    @pl.when(pl.program_id(1) == 0)
    def _():
        c_ref[...] = jnp.zeros(c_ref.shape, F32)
        n_ref[...] = jnp.zeros(n_ref.shape, F32)
        m_ref[...] = jnp.zeros(m_ref.shape, F32)

    ti = lax.broadcasted_iota(jnp.int32, (Lc, Lc), 0)
    si = lax.broadcasted_iota(jnp.int32, (Lc, Lc), 1)
    causal = si <= ti

    def chunk(c, carry):
        r0 = pl.multiple_of(c * Lc, Lc)
        for b in range(nb):
            log_i = g_ref[b, pl.ds(r0, Lc), 0:LANES]
            log_f = g_ref[b, pl.ds(r0, Lc), LANES:2 * LANES]
            m_prev = m_ref[b, 0:1, :]
            bcum = _scan_sublanes(log_f, jnp.add, 0.0)
            g = bcum[Lc - 1:Lc, :]
            e = log_i - bcum
            m_t = bcum + jnp.maximum(m_prev, _scan_sublanes(e, jnp.maximum, NEG))
            c_minus_m = bcum - m_t
            inter_all = jnp.exp(c_minus_m + m_prev)
            floor_all = jnp.exp(-m_t)
            a = g + e
            m_new = jnp.maximum(g + m_prev, jnp.max(a, axis=0, keepdims=True))
            decay_all = jnp.exp(g + m_prev - m_new)
            w_all = jnp.exp(a - m_new)
            e_rows = jnp.transpose(e)
            m_ref[b, 0:1, :] = m_new
            for h in range(H):
                s = b * H + h
                q = q_ref[h, b, pl.ds(r0, Lc), :]
                k = k_ref[h, b, pl.ds(r0, Lc), :]
                v = v_ref[h, b, pl.ds(r0, Lc), :]
                dprime = jnp.where(causal, c_minus_m[:, h:h + 1] + e_rows[h:h + 1, :], NEG)
                qk = lax.dot_general(q, k, (((1,), (1,)), ((), ())), preferred_element_type=F32)
                scores = qk * jnp.exp(dprime)
                inter = inter_all[:, h:h + 1]
                qc = jnp.dot(q, c_ref[s].astype(BF16), preferred_element_type=F32)
                num = jnp.dot(scores.astype(BF16), v, preferred_element_type=F32) + inter * qc
                qn = jnp.sum(q.astype(F32) * n_ref[s, 0:1, :], axis=1, keepdims=True)
                den = jnp.sum(scores, axis=1, keepdims=True) + inter * qn
                hh = num / jnp.maximum(jnp.abs(den), floor_all[:, h:h + 1])
                hh = _rms(hh, nw_ref[h])
                gate = jax.nn.sigmoid(o_ref[h, b, pl.ds(r0, Lc), :].astype(F32))
                h_ref[h, b, pl.ds(r0, Lc), :] = (gate * hh).astype(BF16)

                decay = decay_all[:, h:h + 1]
                kw = k.astype(F32) * w_all[:, h:h + 1]
                dc = lax.dot_general(kw.astype(BF16), v, (((0,), (0,)), ((), ())), preferred_element_type=F32)
                c_ref[s] = decay * c_ref[s] + dc
                n_ref[s, 0:1, :] = decay * n_ref[s, 0:1, :] + jnp.sum(kw, axis=0, keepdims=True)
        return carry

    lax.fori_loop(0, ts // Lc, chunk, 0)


def _mlstm(mq, mk, mv, mo, gates, norm_w, batch, seq_len):
    H, T, D = mq.shape
    ts = MLSTM_TILE
    nb = MLSTM_BATCHES if batch % MLSTM_BATCHES == 0 else 1
    assert seq_len % ts == 0
    nt = seq_len // ts
    per_seq = lambda t: t.reshape(H, batch, seq_len, D)
    hspec = pl.BlockSpec((H, nb, ts, D), lambda b, j: (0, b, j, 0))
    out = pl.pallas_call(
        _mlstm_kernel,
        grid=(batch // nb, nt),
        in_specs=[hspec, hspec, hspec, hspec,
                  pl.BlockSpec((nb, ts, 2 * LANES), lambda b, j: (b, j, 0)),
                  pl.BlockSpec((H, 1, D), lambda b, j: (0, 0, 0))],
        out_specs=hspec,
        out_shape=jax.ShapeDtypeStruct((H, batch, seq_len, D), BF16),
        scratch_shapes=[pltpu.VMEM((nb * H, D, D), F32), pltpu.VMEM((nb * H, HALO, D), F32),
                        pltpu.VMEM((nb, HALO, LANES), F32)],
        compiler_params=pltpu.CompilerParams(
            dimension_semantics=("arbitrary", "arbitrary"), vmem_limit_bytes=VMEM_LIMIT),
        name="mlstm",
    )(per_seq(mq), per_seq(mk), per_seq(mv), per_seq(mo), gates.reshape(batch, seq_len, 2 * LANES),
      norm_w.reshape(H, 1, D))
    return out.reshape(H, T, D)


def _ffn_kernel(x_ref, att_ref, hm_ref, wout_ref, fnw_ref, wup_ref, cw_ref, cb_ref, wdown_ref, finw_ref,
                o_ref, us_ref, *, tiles_per_seq, final_norm):
    tm = x_ref.shape[0]
    nf, _, fc2 = wup_ref.shape
    fc = fc2 // 2
    i = pl.program_id(0)

    y = jnp.concatenate([att_ref[p] for p in range(ATT_PAIRS)] + [hm_ref[h] for h in range(MLSTM_HEADS)], axis=1)
    h1 = x_ref[...] + jnp.dot(y, wout_ref[...], preferred_element_type=F32)
    o_ref[...] = h1
    xn = _rms(h1, fnw_ref[...]).astype(BF16)

    @pl.when(i % tiles_per_seq == 0)
    def _():
        us_ref[:, 0:HALO, :] = jnp.zeros((nf, HALO, fc2), F32)

    def up(c):
        us_ref[c, HALO:HALO + tm, :] = jnp.dot(xn, wup_ref[c], preferred_element_type=F32)

    def activation(c):
        u = cb_ref[c]
        for j in range(FFN_CONV):
            off = HALO - (FFN_CONV - 1) + j
            u = u + us_ref[c, off:off + tm, :] * cw_ref[c, j:j + 1, :]
        us_ref[c, 0:HALO, :] = us_ref[c, tm:tm + HALO, :]
        gate, val = u[:, :fc], u[:, fc:]
        return (gate / (1.0 + jnp.exp(-gate)) * val).astype(BF16)

    up(0)
    for c in range(nf):
        if c + 1 < nf:
            up(c + 1)
        o_ref[...] += jnp.dot(activation(c), wdown_ref[c], preferred_element_type=F32)
    if final_norm:
        o_ref[...] = _rms(o_ref[...], finw_ref[...])


def _out_ffn(x2d, att, hm, w_out, ffn_norm_w, w_up, conv_w, conv_b, w_down, final_w, seq_len, final_norm):
    T, D = x2d.shape
    tm = FFN_TILE
    assert T % tm == 0 and seq_len % tm == 0
    nf, _, fc2 = w_up.shape
    row = lambda i: (i, 0)
    const2 = lambda i: (0, 0)
    const3 = lambda i: (0, 0, 0)
    blk3 = lambda i: (0, i, 0)
    once = dict(pipeline_mode=pl.Buffered(1))
    return pl.pallas_call(
        functools.partial(_ffn_kernel, tiles_per_seq=seq_len // tm, final_norm=final_norm),
        grid=(T // tm,),
        in_specs=[
            pl.BlockSpec((tm, D), row),
            pl.BlockSpec((ATT_PAIRS, tm, LANES), blk3),
            pl.BlockSpec((MLSTM_HEADS, tm, MLSTM_HEAD_DIM), blk3),
            pl.BlockSpec(w_out.shape, const2, **once),
            pl.BlockSpec((1, D), const2),
            pl.BlockSpec(w_up.shape, const3, **once),
            pl.BlockSpec(conv_w.shape, const3),
            pl.BlockSpec(conv_b.shape, const3),
            pl.BlockSpec(w_down.shape, const3, **once),
            pl.BlockSpec((1, D), const2),
        ],
        out_specs=pl.BlockSpec((tm, D), row),
        out_shape=jax.ShapeDtypeStruct((T, D), F32),
        scratch_shapes=[pltpu.VMEM((nf, HALO + tm, fc2), F32)],
        compiler_params=pltpu.CompilerParams(
            dimension_semantics=("arbitrary",), vmem_limit_bytes=VMEM_LIMIT),
        name="out_proj_conv_ffn",
    )(x2d, att, hm, w_out, ffn_norm_w, w_up, conv_w, conv_b, w_down, final_w)


def _prep_ffn_weights(w_up, conv_w, conv_b, w_down):
    d_model, two_f = w_up.shape
    f = two_f // 2
    fc = FFN_CHUNK
    assert f % fc == 0
    nf = f // fc

    def regroup(t):
        g = t[..., :f].reshape(t.shape[:-1] + (nf, fc))
        v = t[..., f:].reshape(t.shape[:-1] + (nf, fc))
        return jnp.moveaxis(jnp.concatenate([g, v], axis=-1), -2, 0)

    return (regroup(w_up).astype(BF16), regroup(conv_w), regroup(conv_b[None, :]),
            w_down.reshape(nf, fc, d_model).astype(BF16))


def kernel(x, w_in, mlstm_conv_w, mlstm_conv_b, mlstm_i_bias, mlstm_f_bias, att_out_norm_w, mlstm_out_norm_w, w_out, mixer_norm_w, ffn_norm_w, w_ffn_up, ffn_conv_w, ffn_conv_b, w_ffn_down, final_norm_w):
    batch, seq_len, d_model = x.shape
    depth = w_in.shape[0]
    h = x.reshape(batch * seq_len, d_model)
    for layer in range(depth):
        w = w_in[layer]
        n_gate = 2 * MLSTM_HEADS
        w_pad = jnp.pad(w, ((0, 0), (0, LANES - n_gate))).astype(BF16)
        gate_b = jnp.pad(jnp.concatenate([mlstm_i_bias[layer], mlstm_f_bias[layer]]), (0, LANES - n_gate))[None, :]
        aq, ak, av, mq, mk, mv, mo, gates = _input_projection(
            h, mixer_norm_w[layer][None, :], w_pad, mlstm_conv_w[layer], mlstm_conv_b[layer][None, :],
            gate_b, seq_len)
        att = _dilated_attention(aq, ak, av, att_out_norm_w[layer], batch, seq_len)
        hm = _mlstm(mq, mk, mv, mo, gates, mlstm_out_norm_w[layer], batch, seq_len)
        w_up, cw, cb, w_down = _prep_ffn_weights(w_ffn_up[layer], ffn_conv_w[layer], ffn_conv_b[layer],
                                                 w_ffn_down[layer])
        h = _out_ffn(h, att, hm, w_out[layer].astype(BF16), ffn_norm_w[layer][None, :], w_up, cw, cb, w_down,
                     final_norm_w[None, :], seq_len, final_norm=(layer == depth - 1))
    return h.reshape(batch, seq_len, d_model)
```

```python
import functools

import numpy as np
import jax
import jax.numpy as jnp
from jax import lax
from jax.experimental import pallas as pl
from jax.experimental.pallas import tpu as pltpu

F32 = jnp.float32
BF16 = jnp.bfloat16

EPS = 1e-6
LANES = 128
ATT_HEADS = 8
ATT_HEAD_DIM = 64
ATT_WIDTH = ATT_HEADS * ATT_HEAD_DIM
ATT_PAIRS = ATT_WIDTH // LANES
ATT_BLOCK = 128
DILATIONS = (16, 4, 1)
MAX_DIL = 16
ATT_TILE = ATT_BLOCK * MAX_DIL
MLSTM_HEADS = 4
MLSTM_HEAD_DIM = 128
MLSTM_WIDTH = MLSTM_HEADS * MLSTM_HEAD_DIM
MLSTM_CHUNK = 64
MLSTM_CONV = 4
FFN_CONV = 3
HALO = 8
NEG = -1e30
LOG2E = 1.4426950408889634
VMEM_LIMIT = 56 * 1024 * 1024

PROJ_TILE = 512
FFN_TILE = 512
FFN_CHUNK = 256
MLSTM_TILE = 512
MLSTM_BATCHES = 4


def _rms(x, w):
    return x * lax.rsqrt(jnp.mean(x * x, axis=-1, keepdims=True) + EPS) * w


def _dilation_permutation(tile):
    rows = tile // MAX_DIL
    n = np.arange(tile)
    p = np.zeros((tile, tile), np.float32)
    p[n, MAX_DIL * (n % rows) + n // rows] = 1.0
    return p


def _proj_kernel(x_ref, nw_ref, w_ref, perm_ref, cw_ref, cb_ref, gb_ref,
                 aq_ref, ak_ref, av_ref, mq_ref, mk_ref, mv_ref, mo_ref, g_ref,
                 cs_ref, *, tiles_per_seq):
    tm = x_ref.shape[0]
    i = pl.program_id(0)
    xn = _rms(x_ref[...], nw_ref[...]).astype(BF16)
    xp = jnp.dot(perm_ref[...], xn, preferred_element_type=F32).astype(BF16)
    base = 3 * ATT_WIDTH

    def seg(lhs, c0, width):
        return lambda: jnp.dot(lhs, w_ref[:, c0:c0 + width], preferred_element_type=F32)

    def put_blocks(ref, y):
        for p in range(ref.shape[0]):
            ref[p] = y[:, LANES * p:LANES * (p + 1)].astype(BF16)

    def put_view(ref, y):
        rows = tm // MAX_DIL
        for p in range(ATT_PAIRS):
            for r in range(MAX_DIL):
                ref[p, :, LANES * r:LANES * (r + 1)] = y[rows * r:rows * (r + 1), LANES * p:LANES * (p + 1)].astype(BF16)

    @pl.when(i % tiles_per_seq == 0)
    def _():
        cs_ref[0:HALO, :] = jnp.zeros((HALO, cs_ref.shape[1]), F32)

    def conv_silu(y):
        cs_ref[HALO:HALO + tm, :] = y
        y = cb_ref[...]
        for j in range(MLSTM_CONV):
            off = HALO - (MLSTM_CONV - 1) + j
            y = y + cs_ref[off:off + tm, :] * cw_ref[j:j + 1, :]
        cs_ref[0:HALO, :] = cs_ref[tm:tm + HALO, :]
        y = y / (1.0 + jnp.exp(-y))
        put_blocks(mq_ref, y[:, :MLSTM_WIDTH])
        put_blocks(mk_ref, y[:, MLSTM_WIDTH:] * (MLSTM_HEAD_DIM ** -0.5))

    def gates(z):
        z = z + gb_ref[...]
        g_ref[:, 0:LANES] = z
        log_sig = jnp.minimum(z, 0.0) - jnp.log1p(jnp.exp(-jnp.abs(z)))
        g_ref[:, LANES:2 * LANES] = pltpu.roll(log_sig, LANES - MLSTM_HEADS, axis=1)

    stages = [
        (seg(xp, 0, ATT_WIDTH), lambda y: put_view(aq_ref, y * (ATT_HEAD_DIM ** -0.5 * LOG2E))),
        (seg(xp, ATT_WIDTH, ATT_WIDTH), lambda y: put_view(ak_ref, y)),
        (seg(xp, 2 * ATT_WIDTH, ATT_WIDTH), lambda y: put_view(av_ref, y)),
        (seg(xn, base, 2 * MLSTM_WIDTH), conv_silu),
        (seg(xn, base + 2 * MLSTM_WIDTH, MLSTM_WIDTH), lambda y: put_blocks(mv_ref, y)),
        (seg(xn, base + 3 * MLSTM_WIDTH, MLSTM_WIDTH), lambda y: put_blocks(mo_ref, y)),
        (seg(xn, base + 4 * MLSTM_WIDTH, LANES), gates),
    ]
    y_next = stages[0][0]()
    for n, (_, epilogue) in enumerate(stages):
        y = y_next
        if n + 1 < len(stages):
            y_next = stages[n + 1][0]()
        epilogue(y)


def _input_projection(x2d, norm_w, w_pad, conv_w, conv_b, gate_b, seq_len):
    T, D = x2d.shape
    tm = PROJ_TILE
    assert T % tm == 0 and seq_len % tm == 0
    row = lambda i: (i, 0)
    const2 = lambda i: (0, 0)
    blk3 = lambda i: (0, i, 0)
    perm = jnp.asarray(_dilation_permutation(tm), BF16)
    heads_shape = jax.ShapeDtypeStruct((MLSTM_HEADS, T, LANES), BF16)
    heads_spec = pl.BlockSpec((MLSTM_HEADS, tm, LANES), blk3)
    view_shape = jax.ShapeDtypeStruct((ATT_PAIRS, T // MAX_DIL, MAX_DIL * LANES), BF16)
    view_spec = pl.BlockSpec((ATT_PAIRS, tm // MAX_DIL, MAX_DIL * LANES), blk3)
    return pl.pallas_call(
        functools.partial(_proj_kernel, tiles_per_seq=seq_len // tm),
        grid=(T // tm,),
        in_specs=[
            pl.BlockSpec((tm, D), row),
            pl.BlockSpec((1, D), const2),
            pl.BlockSpec(w_pad.shape, const2),
            pl.BlockSpec(perm.shape, const2),
            pl.BlockSpec(conv_w.shape, const2),
            pl.BlockSpec(conv_b.shape, const2),
            pl.BlockSpec(gate_b.shape, const2),
        ],
        out_specs=[view_spec] * 3 + [heads_spec] * 4 + [pl.BlockSpec((tm, 2 * LANES), row)],
        out_shape=[view_shape] * 3 + [heads_shape] * 4 + [jax.ShapeDtypeStruct((T, 2 * LANES), F32)],
        scratch_shapes=[pltpu.VMEM((HALO + tm, 2 * MLSTM_WIDTH), F32)],
        compiler_params=pltpu.CompilerParams(
            dimension_semantics=("arbitrary",), vmem_limit_bytes=VMEM_LIMIT),
        name="input_projection",
    )(x2d, norm_w, w_pad, perm, conv_w, conv_b, gate_b)


def _attention_biases():
    n = np.arange(ATT_BLOCK)
    s = np.arange(2 * ATT_BLOCK)
    out = np.zeros((3, 2, 2 * ATT_BLOCK, 2 * ATT_BLOCK), np.float32)
    pq16, pk16 = n, s
    pq4 = 4 * (n % 32) + n // 32
    pk4 = 4 * (s % 64) + s // 64
    perm1 = 16 * (n % 8) + n // 8
    pq1 = perm1
    pk1 = perm1[s % ATT_BLOCK] + ATT_BLOCK * (s // ATT_BLOCK)
    shifts = (ATT_BLOCK, 4 * 32, ATT_BLOCK)
    for g, (pq, pk, shift) in enumerate(((pq16, pk16, shifts[0]), (pq4, pk4, shifts[1]), (pq1, pk1, shifts[2]))):
        for variant, sh in enumerate((shift, 0)):
            dist = pq[:, None] - (pk[None, :] - sh)
            ok = (dist >= 0) & (dist <= ATT_BLOCK)
            bias = np.where(ok, 0.0, NEG).astype(np.float32)
            out[g, variant] = np.concatenate([bias, bias], axis=0)
    return out


def _attn_unit(q, kk, vv, bias, half0):
    zero = jnp.zeros_like(q)
    q2 = jnp.concatenate([jnp.where(half0, q, zero), jnp.where(half0, zero, q)], axis=0)
    s = lax.dot_general(q2, kk, (((1,), (1,)), ((), ())), preferred_element_type=F32) + bias
    m = jnp.max(s, axis=-1, keepdims=True)
    p = jnp.exp2(s - m).astype(BF16)
    vext = jnp.concatenate([vv, jnp.ones_like(vv)], axis=1)
    res = jnp.dot(p, vext, preferred_element_type=F32)
    top, bot = res[:ATT_BLOCK], res[ATT_BLOCK:]
    o = jnp.where(half0, top[:, :LANES], bot[:, :LANES])
    den = jnp.where(half0, top[:, LANES:], bot[:, LANES:])
    mb = jnp.where(half0, m[:ATT_BLOCK], m[ATT_BLOCK:])
    return o, mb, den


def _attention_kernel(q_ref, k_ref, v_ref, bias_ref, nw_ref, o_ref, qf_ref, kf_ref, vf_ref, st_ref):
    jt = pl.program_id(2)
    half0 = lax.broadcasted_iota(jnp.int32, (ATT_BLOCK, LANES), 1) < ATT_HEAD_DIM
    row0 = pl.multiple_of(jt * ATT_BLOCK, ATT_BLOCK)

    def lanes(r):
        return slice(LANES * r, LANES * (r + 1))

    @pl.when(jt == 0)
    def _():
        qf_ref[...] = q_ref[...].astype(F32)
        kf_ref[...] = k_ref[...].astype(F32)
        vf_ref[...] = v_ref[...].astype(F32)

    def store_stats(g, r, rows, stats):
        for st, val in enumerate(stats):
            st_ref[g, st, r, rows, :] = val

    first16 = jt == 0
    k0 = pl.multiple_of(jnp.maximum(row0 - ATT_BLOCK, 0), ATT_BLOCK)
    bias16 = bias_ref[0, jnp.where(first16, 1, 0)]
    for r in range(MAX_DIL):
        q = q_ref[pl.ds(row0, ATT_BLOCK), lanes(r)]
        kk = k_ref[pl.ds(k0, 2 * ATT_BLOCK), lanes(r)]
        vv = v_ref[pl.ds(k0, 2 * ATT_BLOCK), lanes(r)]
        store_stats(0, r, slice(None), _attn_unit(q, kk, vv, bias16, half0))

    def body4(bb, carry):
        gb = jt * 4 + bb
        q0 = pl.multiple_of(gb * 32, 32)
        k0 = pl.multiple_of(jnp.maximum(gb * 32 - 32, 0), 32)
        bias = bias_ref[1, jnp.where(gb == 0, 1, 0)]
        o0 = pl.multiple_of(bb * 32, 32)
        for r0 in range(4):
            q = jnp.concatenate([q_ref[pl.ds(q0, 32), lanes(4 * r1 + r0)] for r1 in range(4)], axis=0)
            kk = jnp.concatenate([k_ref[pl.ds(k0, 64), lanes(4 * r1 + r0)] for r1 in range(4)], axis=0)
            vv = jnp.concatenate([v_ref[pl.ds(k0, 64), lanes(4 * r1 + r0)] for r1 in range(4)], axis=0)
            stats = _attn_unit(q, kk, vv, bias, half0)
            for r1 in range(4):
                store_stats(1, 4 * r1 + r0, pl.ds(o0, 32), [s[32 * r1:32 * (r1 + 1)] for s in stats])
        return carry

    lax.fori_loop(0, 4, body4, 0, unroll=2)

    def body1(bl, carry):
        b = jt * MAX_DIL + bl
        q0 = pl.multiple_of(b * 8, 8)
        k0 = pl.multiple_of(jnp.maximum(b * 8 - 8, 0), 8)
        bias = bias_ref[2, jnp.where(b == 0, 1, 0)]
        o0 = pl.multiple_of(bl * 8, 8)

        def gather(ref, start):
            return jnp.concatenate([ref[pl.ds(start, 8), lanes(r)] for r in range(MAX_DIL)], axis=0)

        q = gather(qf_ref, q0).astype(BF16)
        kk = jnp.concatenate([gather(kf_ref, k0), gather(kf_ref, k0 + 8)], axis=0).astype(BF16)
        vv = jnp.concatenate([gather(vf_ref, k0), gather(vf_ref, k0 + 8)], axis=0).astype(BF16)
        stats = _attn_unit(q, kk, vv, bias, half0)
        for r in range(MAX_DIL):
            store_stats(2, r, pl.ds(o0, 8), [s[8 * r:8 * (r + 1)] for s in stats])
        return carry

    lax.fori_loop(0, MAX_DIL, body1, 0, unroll=8)

    nw = nw_ref[...]
    for r in range(MAX_DIL):
        ms = [st_ref[g, 1, r] for g in range(3)]
        mx = jnp.maximum(jnp.maximum(ms[0], ms[1]), ms[2])
        fs = [jnp.exp2(m - mx) for m in ms]
        num = fs[0] * st_ref[0, 0, r] + fs[1] * st_ref[1, 0, r] + fs[2] * st_ref[2, 0, r]
        den = fs[0] * st_ref[0, 2, r] + fs[1] * st_ref[1, 2, r] + fs[2] * st_ref[2, 2, r]
        att = num / den
        a2 = att * att
        s0 = jnp.sum(jnp.where(half0, a2, 0.0), axis=-1, keepdims=True)
        s1 = jnp.sum(jnp.where(half0, 0.0, a2), axis=-1, keepdims=True)
        msq = jnp.where(half0, s0, s1) * (1.0 / ATT_HEAD_DIM)
        o_ref[pl.ds(row0, ATT_BLOCK), lanes(r)] = (att * lax.rsqrt(msq + EPS) * nw).astype(BF16)


def _dilated_attention(aq, ak, av, norm_w, batch, seq_len):
    assert seq_len % ATT_TILE == 0 and seq_len >= 2 * ATT_TILE
    rows = seq_len // MAX_DIL
    nt = seq_len // ATT_TILE
    view = lambda t: t.reshape(ATT_PAIRS, batch, rows, MAX_DIL * LANES)
    bias = jnp.asarray(_attention_biases())
    seq_spec = pl.BlockSpec((None, None, rows, MAX_DIL * LANES), lambda b, p, j: (p, b, 0, 0))
    out = pl.pallas_call(
        _attention_kernel,
        grid=(batch, ATT_PAIRS, nt),
        in_specs=[seq_spec, seq_spec, seq_spec,
                  pl.BlockSpec(bias.shape, lambda b, p, j: (0, 0, 0, 0)),
                  pl.BlockSpec((None, 1, LANES), lambda b, p, j: (p, 0, 0))],
        out_specs=seq_spec,
        out_shape=jax.ShapeDtypeStruct((ATT_PAIRS, batch, rows, MAX_DIL * LANES), BF16),
        scratch_shapes=[pltpu.VMEM((rows, MAX_DIL * LANES), F32)] * 3
        + [pltpu.VMEM((3, 3, MAX_DIL, ATT_BLOCK, LANES), F32)],
        compiler_params=pltpu.CompilerParams(
            dimension_semantics=("arbitrary", "arbitrary", "arbitrary"), vmem_limit_bytes=VMEM_LIMIT),
        name="dilated_attention",
    )(view(aq), view(ak), view(av), bias, norm_w.reshape(ATT_PAIRS, 1, LANES))
    return out.reshape(ATT_PAIRS, batch * rows, MAX_DIL * LANES)


def _scan_sublanes(x, op, fill):
    n = x.shape[0]
    row = lax.broadcasted_iota(jnp.int32, x.shape, 0)
    sh = 1
    while sh < n:
        x = op(x, jnp.where(row >= sh, pltpu.roll(x, sh, axis=0), fill))
        sh *= 2
    return x


def _mlstm_kernel(q_ref, k_ref, v_ref, o_ref, g_ref, nw_ref, h_ref, c_ref, n_ref, m_ref):
    Lc, D, H = MLSTM_CHUNK, MLSTM_HEAD_DIM, MLSTM_HEADS
    nb, ts = q_ref.shape[1], q_ref.shape[2]

    @pl.when(pl.program_id(1) == 0)
    def _():
        c_ref[...] = jnp.zeros(c_ref.shape, F32)
        n_ref[...] = jnp.zeros(n_ref.shape, F32)
        m_ref[...] = jnp.zeros(m_ref.shape, F32)

    ti = lax.broadcasted_iota(jnp.int32, (Lc, Lc), 0)
    si = lax.broadcasted_iota(jnp.int32, (Lc, Lc), 1)
    causal = si <= ti

    def chunk(c, carry):
        r0 = pl.multiple_of(c * Lc, Lc)
        for b in range(nb):
            log_i = g_ref[b, pl.ds(r0, Lc), 0:LANES]
            log_f = g_ref[b, pl.ds(r0, Lc), LANES:2 * LANES]
            m_prev = m_ref[b, 0:1, :]
            bcum = _scan_sublanes(log_f, jnp.add, 0.0)
            g = bcum[Lc - 1:Lc, :]
            e = log_i - bcum
            m_t = bcum + jnp.maximum(m_prev, _scan_sublanes(e, jnp.maximum, NEG))
            c_minus_m = bcum - m_t
            inter_all = jnp.exp(c_minus_m + m_prev)
            floor_all = jnp.exp(-m_t)
            a = g + e
            m_new = jnp.maximum(g + m_prev, jnp.max(a, axis=0, keepdims=True))
            decay_all = jnp.exp(g + m_prev - m_new)
            w_all = jnp.exp(a - m_new)
            e_rows = jnp.transpose(e)
            m_ref[b, 0:1, :] = m_new
            for h in range(H):
                s = b * H + h
                q = q_ref[h, b, pl.ds(r0, Lc), :]
                k = k_ref[h, b, pl.ds(r0, Lc), :]
                v = v_ref[h, b, pl.ds(r0, Lc), :]
                dprime = jnp.where(causal, c_minus_m[:, h:h + 1] + e_rows[h:h + 1, :], NEG)
                qk = lax.dot_general(q, k, (((1,), (1,)), ((), ())), preferred_element_type=F32)
                scores = qk * jnp.exp(dprime)
                inter = inter_all[:, h:h + 1]
                qc = jnp.dot(q, c_ref[s].astype(BF16), preferred_element_type=F32)
                num = jnp.dot(scores.astype(BF16), v, preferred_element_type=F32) + inter * qc
                qn = jnp.sum(q.astype(F32) * n_ref[s, 0:1, :], axis=1, keepdims=True)
                den = jnp.sum(scores, axis=1, keepdims=True) + inter * qn
                hh = num / jnp.maximum(jnp.abs(den), floor_all[:, h:h + 1])
                hh = _rms(hh, nw_ref[h])
                gate = jax.nn.sigmoid(o_ref[h, b, pl.ds(r0, Lc), :].astype(F32))
                h_ref[h, b, pl.ds(r0, Lc), :] = (gate * hh).astype(BF16)

                decay = decay_all[:, h:h + 1]
                kw = k.astype(F32) * w_all[:, h:h + 1]
                dc = lax.dot_general(kw.astype(BF16), v, (((0,), (0,)), ((), ())), preferred_element_type=F32)
                c_ref[s] = decay * c_ref[s] + dc
                n_ref[s, 0:1, :] = decay * n_ref[s, 0:1, :] + jnp.sum(kw, axis=0, keepdims=True)
        return carry

    lax.fori_loop(0, ts // Lc, chunk, 0)


def _mlstm(mq, mk, mv, mo, gates, norm_w, batch, seq_len):
    H, T, D = mq.shape
    ts = MLSTM_TILE
    nb = MLSTM_BATCHES if batch % MLSTM_BATCHES == 0 else 1
    assert seq_len % ts == 0
    nt = seq_len // ts
    per_seq = lambda t: t.reshape(H, batch, seq_len, D)
    hspec = pl.BlockSpec((H, nb, ts, D), lambda b, j: (0, b, j, 0))
    out = pl.pallas_call(
        _mlstm_kernel,
        grid=(batch // nb, nt),
        in_specs=[hspec, hspec, hspec, hspec,
                  pl.BlockSpec((nb, ts, 2 * LANES), lambda b, j: (b, j, 0)),
                  pl.BlockSpec((H, 1, D), lambda b, j: (0, 0, 0))],
        out_specs=hspec,
        out_shape=jax.ShapeDtypeStruct((H, batch, seq_len, D), BF16),
        scratch_shapes=[pltpu.VMEM((nb * H, D, D), F32), pltpu.VMEM((nb * H, HALO, D), F32),
                        pltpu.VMEM((nb, HALO, LANES), F32)],
        compiler_params=pltpu.CompilerParams(
            dimension_semantics=("arbitrary", "arbitrary"), vmem_limit_bytes=VMEM_LIMIT),
        name="mlstm",
    )(per_seq(mq), per_seq(mk), per_seq(mv), per_seq(mo), gates.reshape(batch, seq_len, 2 * LANES),
      norm_w.reshape(H, 1, D))
    return out.reshape(H, T, D)


def _ffn_kernel(x_ref, att_ref, hm_ref, permt_ref, wout_ref, fnw_ref, wup_ref, cw_ref, cb_ref, wdown_ref, finw_ref,
                o_ref, us_ref, *, tiles_per_seq, final_norm):
    tm = x_ref.shape[0]
    nf, _, fc2 = wup_ref.shape
    fc = fc2 // 2
    i = pl.program_id(0)

    att = jnp.concatenate(
        [jnp.concatenate([att_ref[p, :, LANES * r:LANES * (r + 1)] for r in range(MAX_DIL)], axis=0)
         for p in range(ATT_PAIRS)], axis=1)
    att = jnp.dot(permt_ref[...], att, preferred_element_type=F32).astype(BF16)
    y = jnp.concatenate([att] + [hm_ref[h] for h in range(MLSTM_HEADS)], axis=1)
    h1 = x_ref[...] + jnp.dot(y, wout_ref[...], preferred_element_type=F32)
    o_ref[...] = h1
    xn = _rms(h1, fnw_ref[...]).astype(BF16)

    @pl.when(i % tiles_per_seq == 0)
    def _():
        us_ref[:, 0:HALO, :] = jnp.zeros((nf, HALO, fc2), F32)

    def up(c):
        us_ref[c, HALO:HALO + tm, :] = jnp.dot(xn, wup_ref[c], preferred_element_type=F32)

    def activation(c):
        u = cb_ref[c]
        for j in range(FFN_CONV):
            off = HALO - (FFN_CONV - 1) + j
            u = u + us_ref[c, off:off + tm, :] * cw_ref[c, j:j + 1, :]
        us_ref[c, 0:HALO, :] = us_ref[c, tm:tm + HALO, :]
        gate, val = u[:, :fc], u[:, fc:]
        return (gate / (1.0 + jnp.exp(-gate)) * val).astype(BF16)

    def down(c, act):
        o_ref[...] += jnp.dot(act, wdown_ref[c], preferred_element_type=F32)

    up(0)
    if nf > 1:
        up(1)
    act = activation(0)
    for c in range(1, nf):
        if c + 1 < nf:
            up(c + 1)
        down(c - 1, act)
        act = activation(c)
    down(nf - 1, act)
    if final_norm:
        o_ref[...] = _rms(o_ref[...], finw_ref[...])


def _out_ffn(x2d, att, hm, w_out, ffn_norm_w, w_up, conv_w, conv_b, w_down, final_w, seq_len, final_norm):
    T, D = x2d.shape
    tm = FFN_TILE
    assert T % tm == 0 and seq_len % tm == 0
    nf, _, fc2 = w_up.shape
    row = lambda i: (i, 0)
    const2 = lambda i: (0, 0)
    const3 = lambda i: (0, 0, 0)
    blk3 = lambda i: (0, i, 0)
    once = dict(pipeline_mode=pl.Buffered(1))
    permt = jnp.asarray(_dilation_permutation(tm).T, BF16)
    return pl.pallas_call(
        functools.partial(_ffn_kernel, tiles_per_seq=seq_len // tm, final_norm=final_norm),
        grid=(T // tm,),
        in_specs=[
            pl.BlockSpec((tm, D), row),
            pl.BlockSpec((ATT_PAIRS, tm // MAX_DIL, MAX_DIL * LANES), blk3),
            pl.BlockSpec((MLSTM_HEADS, tm, MLSTM_HEAD_DIM), blk3),
            pl.BlockSpec(permt.shape, const2),
            pl.BlockSpec(w_out.shape, const2, **once),
            pl.BlockSpec((1, D), const2),
            pl.BlockSpec(w_up.shape, const3, **once),
            pl.BlockSpec(conv_w.shape, const3),
            pl.BlockSpec(conv_b.shape, const3),
            pl.BlockSpec(w_down.shape, const3, **once),
            pl.BlockSpec((1, D), const2),
        ],
        out_specs=pl.BlockSpec((tm, D), row),
        out_shape=jax.ShapeDtypeStruct((T, D), F32),
        scratch_shapes=[pltpu.VMEM((nf, HALO + tm, fc2), F32)],
        compiler_params=pltpu.CompilerParams(
            dimension_semantics=("arbitrary",), vmem_limit_bytes=VMEM_LIMIT),
        name="out_proj_conv_ffn",
    )(x2d, att, hm, permt, w_out, ffn_norm_w, w_up, conv_w, conv_b, w_down, final_w)


def _prep_ffn_weights(w_up, conv_w, conv_b, w_down):
    d_model, two_f = w_up.shape
    f = two_f // 2
    fc = FFN_CHUNK
    assert f % fc == 0
    nf = f // fc

    def regroup(t):
        g = t[..., :f].reshape(t.shape[:-1] + (nf, fc))
        v = t[..., f:].reshape(t.shape[:-1] + (nf, fc))
        return jnp.moveaxis(jnp.concatenate([g, v], axis=-1), -2, 0)

    return (regroup(w_up).astype(BF16), regroup(conv_w), regroup(conv_b[None, :]),
            w_down.reshape(nf, fc, d_model).astype(BF16))


def kernel(x, w_in, mlstm_conv_w, mlstm_conv_b, mlstm_i_bias, mlstm_f_bias, att_out_norm_w, mlstm_out_norm_w, w_out, mixer_norm_w, ffn_norm_w, w_ffn_up, ffn_conv_w, ffn_conv_b, w_ffn_down, final_norm_w):
    batch, seq_len, d_model = x.shape
    depth = w_in.shape[0]
    h = x.reshape(batch * seq_len, d_model)
    for layer in range(depth):
        w = w_in[layer]
        n_gate = 2 * MLSTM_HEADS
        w_pad = jnp.pad(w, ((0, 0), (0, LANES - n_gate))).astype(BF16)
        gate_b = jnp.pad(jnp.concatenate([mlstm_i_bias[layer], mlstm_f_bias[layer]]), (0, LANES - n_gate))[None, :]
        aq, ak, av, mq, mk, mv, mo, gates = _input_projection(
            h, mixer_norm_w[layer][None, :], w_pad, mlstm_conv_w[layer], mlstm_conv_b[layer][None, :],
            gate_b, seq_len)
        att = _dilated_attention(aq, ak, av, att_out_norm_w[layer], batch, seq_len)
        hm = _mlstm(mq, mk, mv, mo, gates, mlstm_out_norm_w[layer], batch, seq_len)
        w_up, cw, cb, w_down = _prep_ffn_weights(w_ffn_up[layer], ffn_conv_w[layer], ffn_conv_b[layer],
                                                 w_ffn_down[layer])
        h = _out_ffn(h, att, hm, w_out[layer].astype(BF16), ffn_norm_w[layer][None, :], w_up, cw, cb, w_down,
                     final_norm_w[None, :], seq_len, final_norm=(layer == depth - 1))
    return h.reshape(batch, seq_len, d_model)
```

```python
import functools

import numpy as np
import jax
import jax.numpy as jnp
from jax import lax
from jax.experimental import pallas as pl
from jax.experimental.pallas import tpu as pltpu

F32 = jnp.float32
BF16 = jnp.bfloat16

EPS = 1e-6
LANES = 128
ATT_HEADS = 8
ATT_HEAD_DIM = 64
ATT_WIDTH = ATT_HEADS * ATT_HEAD_DIM
ATT_PAIRS = ATT_WIDTH // LANES
ATT_BLOCK = 128
DILATIONS = (16, 4, 1)
MAX_DIL = 16
ATT_TILE = ATT_BLOCK * MAX_DIL
MLSTM_HEADS = 4
MLSTM_HEAD_DIM = 128
MLSTM_WIDTH = MLSTM_HEADS * MLSTM_HEAD_DIM
MLSTM_CHUNK = 64
MLSTM_CONV = 4
FFN_CONV = 3
HALO = 8
NEG = -1e30
LOG2E = 1.4426950408889634
VMEM_LIMIT = 56 * 1024 * 1024

PROJ_TILE = 1024
PERM_TILE = 512
FFN_TILE = 512
FFN_CHUNK = 256
MLSTM_TILE = 512
MLSTM_BATCHES = 4


def _rms(x, w):
    return x * lax.rsqrt(jnp.mean(x * x, axis=-1, keepdims=True) + EPS) * w


def _dilation_permutation(tile):
    rows = tile // MAX_DIL
    n = np.arange(tile)
    p = np.zeros((tile, tile), np.float32)
    p[n, MAX_DIL * (n % rows) + n // rows] = 1.0
    return p


def _proj_kernel(x_ref, nw_ref, w_ref, perm_ref, cw_ref, cb_ref, gb_ref,
                 aq_ref, ak_ref, av_ref, mq_ref, mk_ref, mv_ref, mo_ref, g_ref,
                 cs_ref, *, tiles_per_seq):
    tm = x_ref.shape[0]
    i = pl.program_id(0)
    xn = _rms(x_ref[...], nw_ref[...]).astype(BF16)
    xp = jnp.concatenate(
        [jnp.dot(perm_ref[...], xn[PERM_TILE * g:PERM_TILE * (g + 1)], preferred_element_type=F32).astype(BF16)
         for g in range(tm // PERM_TILE)], axis=0)
    base = 3 * ATT_WIDTH

    def seg(lhs, c0, width):
        return lambda: jnp.dot(lhs, w_ref[:, c0:c0 + width], preferred_element_type=F32)

    def put_blocks(ref, y):
        for p in range(ref.shape[0]):
            ref[p] = y[:, LANES * p:LANES * (p + 1)].astype(BF16)

    def put_view(ref, y):
        rows = PERM_TILE // MAX_DIL
        for g in range(tm // PERM_TILE):
            for p in range(ATT_PAIRS):
                for r in range(MAX_DIL):
                    src = PERM_TILE * g + rows * r
                    ref[p, rows * g:rows * (g + 1), LANES * r:LANES * (r + 1)] = (
                        y[src:src + rows, LANES * p:LANES * (p + 1)].astype(BF16))

    @pl.when(i % tiles_per_seq == 0)
    def _():
        cs_ref[0:HALO, :] = jnp.zeros((HALO, cs_ref.shape[1]), F32)

    def conv_silu(y):
        cs_ref[HALO:HALO + tm, :] = y
        y = cb_ref[...]
        for j in range(MLSTM_CONV):
            off = HALO - (MLSTM_CONV - 1) + j
            y = y + cs_ref[off:off + tm, :] * cw_ref[j:j + 1, :]
        cs_ref[0:HALO, :] = cs_ref[tm:tm + HALO, :]
        y = y / (1.0 + jnp.exp(-y))
        put_blocks(mq_ref, y[:, :MLSTM_WIDTH])
        put_blocks(mk_ref, y[:, MLSTM_WIDTH:] * (MLSTM_HEAD_DIM ** -0.5))

    def gates(z):
        z = z + gb_ref[...]
        g_ref[:, 0:LANES] = z
        log_sig = jnp.minimum(z, 0.0) - jnp.log1p(jnp.exp(-jnp.abs(z)))
        g_ref[:, LANES:2 * LANES] = pltpu.roll(log_sig, LANES - MLSTM_HEADS, axis=1)

    stages = [
        (seg(xp, 0, ATT_WIDTH), lambda y: put_view(aq_ref, y * (ATT_HEAD_DIM ** -0.5 * LOG2E))),
        (seg(xp, ATT_WIDTH, ATT_WIDTH), lambda y: put_view(ak_ref, y)),
        (seg(xp, 2 * ATT_WIDTH, ATT_WIDTH), lambda y: put_view(av_ref, y)),
        (seg(xn, base, 2 * MLSTM_WIDTH), conv_silu),
        (seg(xn, base + 2 * MLSTM_WIDTH, MLSTM_WIDTH), lambda y: put_blocks(mv_ref, y)),
        (seg(xn, base + 3 * MLSTM_WIDTH, MLSTM_WIDTH), lambda y: put_blocks(mo_ref, y)),
        (seg(xn, base + 4 * MLSTM_WIDTH, LANES), gates),
    ]
    y_next = stages[0][0]()
    for n, (_, epilogue) in enumerate(stages):
        y = y_next
        if n + 1 < len(stages):
            y_next = stages[n + 1][0]()
        epilogue(y)


def _input_projection(x2d, norm_w, w_pad, conv_w, conv_b, gate_b, seq_len):
    T, D = x2d.shape
    tm = PROJ_TILE
    assert T % tm == 0 and seq_len % tm == 0 and tm % PERM_TILE == 0
    row = lambda i: (i, 0)
    const2 = lambda i: (0, 0)
    blk3 = lambda i: (0, i, 0)
    perm = jnp.asarray(_dilation_permutation(PERM_TILE), BF16)
    heads_shape = jax.ShapeDtypeStruct((MLSTM_HEADS, T, LANES), BF16)
    heads_spec = pl.BlockSpec((MLSTM_HEADS, tm, LANES), blk3)
    view_shape = jax.ShapeDtypeStruct((ATT_PAIRS, T // MAX_DIL, MAX_DIL * LANES), BF16)
    view_spec = pl.BlockSpec((ATT_PAIRS, tm // MAX_DIL, MAX_DIL * LANES), blk3)
    return pl.pallas_call(
        functools.partial(_proj_kernel, tiles_per_seq=seq_len // tm),
        grid=(T // tm,),
        in_specs=[
            pl.BlockSpec((tm, D), row),
            pl.BlockSpec((1, D), const2),
            pl.BlockSpec(w_pad.shape, const2),
            pl.BlockSpec(perm.shape, const2),
            pl.BlockSpec(conv_w.shape, const2),
            pl.BlockSpec(conv_b.shape, const2),
            pl.BlockSpec(gate_b.shape, const2),
        ],
        out_specs=[view_spec] * 3 + [heads_spec] * 4 + [pl.BlockSpec((tm, 2 * LANES), row)],
        out_shape=[view_shape] * 3 + [heads_shape] * 4 + [jax.ShapeDtypeStruct((T, 2 * LANES), F32)],
        scratch_shapes=[pltpu.VMEM((HALO + tm, 2 * MLSTM_WIDTH), F32)],
        compiler_params=pltpu.CompilerParams(
            dimension_semantics=("arbitrary",), vmem_limit_bytes=VMEM_LIMIT),
        name="input_projection",
    )(x2d, norm_w, w_pad, perm, conv_w, conv_b, gate_b)


def _attention_biases():
    n = np.arange(ATT_BLOCK)
    s = np.arange(2 * ATT_BLOCK)
    out = np.zeros((3, 2, 2 * ATT_BLOCK, 2 * ATT_BLOCK), np.float32)
    pq16, pk16 = n, s
    pq4 = 4 * (n % 32) + n // 32
    pk4 = 4 * (s % 64) + s // 64
    perm1 = 16 * (n % 8) + n // 8
    pq1 = perm1
    pk1 = perm1[s % ATT_BLOCK] + ATT_BLOCK * (s // ATT_BLOCK)
    shifts = (ATT_BLOCK, 4 * 32, ATT_BLOCK)
    for g, (pq, pk, shift) in enumerate(((pq16, pk16, shifts[0]), (pq4, pk4, shifts[1]), (pq1, pk1, shifts[2]))):
        for variant, sh in enumerate((shift, 0)):
            dist = pq[:, None] - (pk[None, :] - sh)
            ok = (dist >= 0) & (dist <= ATT_BLOCK)
            bias = np.where(ok, 0.0, NEG).astype(np.float32)
            out[g, variant] = np.concatenate([bias, bias], axis=0)
    return out


def _attn_unit(q, kk, vv, bias, half0):
    zero = jnp.zeros_like(q)
    q2 = jnp.concatenate([jnp.where(half0, q, zero), jnp.where(half0, zero, q)], axis=0)
    s = lax.dot_general(q2, kk, (((1,), (1,)), ((), ())), preferred_element_type=F32) + bias
    m = jnp.max(s, axis=-1, keepdims=True)
    p = jnp.exp2(s - m).astype(BF16)
    vext = jnp.concatenate([vv, jnp.ones_like(vv)], axis=1)
    res = jnp.dot(p, vext, preferred_element_type=F32)
    top, bot = res[:ATT_BLOCK], res[ATT_BLOCK:]
    o = jnp.where(half0, top[:, :LANES], bot[:, :LANES])
    den = jnp.where(half0, top[:, LANES:], bot[:, LANES:])
    mb = jnp.where(half0, m[:ATT_BLOCK], m[ATT_BLOCK:])
    return o, mb, den


def _attention_kernel(q_ref, k_ref, v_ref, bias_ref, nw_ref, o_ref, qf_ref, kf_ref, vf_ref, st_ref):
    jt = pl.program_id(2)
    half0 = lax.broadcasted_iota(jnp.int32, (ATT_BLOCK, LANES), 1) < ATT_HEAD_DIM
    row0 = pl.multiple_of(jt * ATT_BLOCK, ATT_BLOCK)

    def lanes(r):
        return slice(LANES * r, LANES * (r + 1))

    @pl.when(jt == 0)
    def _():
        qf_ref[...] = q_ref[...].astype(F32)
        kf_ref[...] = k_ref[...].astype(F32)
        vf_ref[...] = v_ref[...].astype(F32)

    def store_stats(g, r, rows, stats):
        for st, val in enumerate(stats):
            st_ref[g, st, r, rows, :] = val

    first16 = jt == 0
    k0 = pl.multiple_of(jnp.maximum(row0 - ATT_BLOCK, 0), ATT_BLOCK)
    bias16 = bias_ref[0, jnp.where(first16, 1, 0)]
    for r in range(MAX_DIL):
        q = q_ref[pl.ds(row0, ATT_BLOCK), lanes(r)]
        kk = k_ref[pl.ds(k0, 2 * ATT_BLOCK), lanes(r)]
        vv = v_ref[pl.ds(k0, 2 * ATT_BLOCK), lanes(r)]
        store_stats(0, r, slice(None), _attn_unit(q, kk, vv, bias16, half0))

    def body4(bb, carry):
        gb = jt * 4 + bb
        q0 = pl.multiple_of(gb * 32, 32)
        k0 = pl.multiple_of(jnp.maximum(gb * 32 - 32, 0), 32)
        bias = bias_ref[1, jnp.where(gb == 0, 1, 0)]
        o0 = pl.multiple_of(bb * 32, 32)
        for r0 in range(4):
            q = jnp.concatenate([q_ref[pl.ds(q0, 32), lanes(4 * r1 + r0)] for r1 in range(4)], axis=0)
            kk = jnp.concatenate([k_ref[pl.ds(k0, 64), lanes(4 * r1 + r0)] for r1 in range(4)], axis=0)
            vv = jnp.concatenate([v_ref[pl.ds(k0, 64), lanes(4 * r1 + r0)] for r1 in range(4)], axis=0)
            stats = _attn_unit(q, kk, vv, bias, half0)
            for r1 in range(4):
                store_stats(1, 4 * r1 + r0, pl.ds(o0, 32), [s[32 * r1:32 * (r1 + 1)] for s in stats])
        return carry

    lax.fori_loop(0, 4, body4, 0, unroll=2)

    def body1(bl, carry):
        b = jt * MAX_DIL + bl
        q0 = pl.multiple_of(b * 8, 8)
        k0 = pl.multiple_of(jnp.maximum(b * 8 - 8, 0), 8)
        bias = bias_ref[2, jnp.where(b == 0, 1, 0)]
        o0 = pl.multiple_of(bl * 8, 8)

        def gather(ref, start):
            return jnp.concatenate([ref[pl.ds(start, 8), lanes(r)] for r in range(MAX_DIL)], axis=0)

        q = gather(qf_ref, q0).astype(BF16)
        kk = jnp.concatenate([gather(kf_ref, k0), gather(kf_ref, k0 + 8)], axis=0).astype(BF16)
        vv = jnp.concatenate([gather(vf_ref, k0), gather(vf_ref, k0 + 8)], axis=0).astype(BF16)
        stats = _attn_unit(q, kk, vv, bias, half0)
        for r in range(MAX_DIL):
            store_stats(2, r, pl.ds(o0, 8), [s[8 * r:8 * (r + 1)] for s in stats])
        return carry

    lax.fori_loop(0, MAX_DIL, body1, 0, unroll=8)

    nw = nw_ref[...]
    for r in range(MAX_DIL):
        ms = [st_ref[g, 1, r] for g in range(3)]
        mx = jnp.maximum(jnp.maximum(ms[0], ms[1]), ms[2])
        fs = [jnp.exp2(m - mx) for m in ms]
        num = fs[0] * st_ref[0, 0, r] + fs[1] * st_ref[1, 0, r] + fs[2] * st_ref[2, 0, r]
        den = fs[0] * st_ref[0, 2, r] + fs[1] * st_ref[1, 2, r] + fs[2] * st_ref[2, 2, r]
        att = num / den
        a2 = att * att
        s0 = jnp.sum(jnp.where(half0, a2, 0.0), axis=-1, keepdims=True)
        s1 = jnp.sum(jnp.where(half0, 0.0, a2), axis=-1, keepdims=True)
        msq = jnp.where(half0, s0, s1) * (1.0 / ATT_HEAD_DIM)
        o_ref[pl.ds(row0, ATT_BLOCK), lanes(r)] = (att * lax.rsqrt(msq + EPS) * nw).astype(BF16)


def _dilated_attention(aq, ak, av, norm_w, batch, seq_len):
    assert seq_len % ATT_TILE == 0 and seq_len >= 2 * ATT_TILE
    rows = seq_len // MAX_DIL
    nt = seq_len // ATT_TILE
    view = lambda t: t.reshape(ATT_PAIRS, batch, rows, MAX_DIL * LANES)
    bias = jnp.asarray(_attention_biases())
    seq_spec = pl.BlockSpec((None, None, rows, MAX_DIL * LANES), lambda b, p, j: (p, b, 0, 0))
    out = pl.pallas_call(
        _attention_kernel,
        grid=(batch, ATT_PAIRS, nt),
        in_specs=[seq_spec, seq_spec, seq_spec,
                  pl.BlockSpec(bias.shape, lambda b, p, j: (0, 0, 0, 0)),
                  pl.BlockSpec((None, 1, LANES), lambda b, p, j: (p, 0, 0))],
        out_specs=seq_spec,
        out_shape=jax.ShapeDtypeStruct((ATT_PAIRS, batch, rows, MAX_DIL * LANES), BF16),
        scratch_shapes=[pltpu.VMEM((rows, MAX_DIL * LANES), F32)] * 3
        + [pltpu.VMEM((3, 3, MAX_DIL, ATT_BLOCK, LANES), F32)],
        compiler_params=pltpu.CompilerParams(
            dimension_semantics=("arbitrary", "arbitrary", "arbitrary"), vmem_limit_bytes=VMEM_LIMIT),
        name="dilated_attention",
    )(view(aq), view(ak), view(av), bias, norm_w.reshape(ATT_PAIRS, 1, LANES))
    return out.reshape(ATT_PAIRS, batch * rows, MAX_DIL * LANES)


def _scan_sublanes(x, op, fill):
    n = x.shape[0]
    row = lax.broadcasted_iota(jnp.int32, x.shape, 0)
    sh = 1
    while sh < n:
        x = op(x, jnp.where(row >= sh, pltpu.roll(x, sh, axis=0), fill))
        sh *= 2
    return x


def _mlstm_kernel(q_ref, k_ref, v_ref, o_ref, g_ref, nw_ref, h_ref, c_ref, n_ref, m_ref):
    Lc, D, H = MLSTM_CHUNK, MLSTM_HEAD_DIM, MLSTM_HEADS
    nb, ts = q_ref.shape[1], q_ref.shape[2]

    @pl.when(pl.program_id(1) == 0)
    def _():
        c_ref[...] = jnp.zeros(c_ref.shape, F32)
        n_ref[...] = jnp.zeros(n_ref.shape, F32)
        m_ref[...] = jnp.zeros(m_ref.shape, F32)

    ti = lax.broadcasted_iota(jnp.int32, (Lc, Lc), 0)
    si = lax.broadcasted_iota(jnp.int32, (Lc, Lc), 1)
    causal = si <= ti

    def chunk(c, carry):
        r0 = pl.multiple_of(c * Lc, Lc)
        for b in range(nb):
            log_i = g_ref[b, pl.ds(r0, Lc), 0:LANES]
            log_f = g_ref[b, pl.ds(r0, Lc), LANES:2 * LANES]
            m_prev = m_ref[b, 0:1, :]
            bcum = _scan_sublanes(log_f, jnp.add, 0.0)
            g = bcum[Lc - 1:Lc, :]
            e = log_i - bcum
            m_t = bcum + jnp.maximum(m_prev, _scan_sublanes(e, jnp.maximum, NEG))
            c_minus_m = bcum - m_t
            inter_all = jnp.exp(c_minus_m + m_prev)
            floor_all = jnp.exp(-m_t)
            a = g + e
            m_new = jnp.maximum(g + m_prev, jnp.max(a, axis=0, keepdims=True))
            decay_all = jnp.exp(g + m_prev - m_new)
            w_all = jnp.exp(a - m_new)
            e_rows = jnp.transpose(e)
            m_ref[b, 0:1, :] = m_new
            for h in range(H):
                s = b * H + h
                q = q_ref[h, b, pl.ds(r0, Lc), :]
                k = k_ref[h, b, pl.ds(r0, Lc), :]
                v = v_ref[h, b, pl.ds(r0, Lc), :]
                dprime = jnp.where(causal, c_minus_m[:, h:h + 1] + e_rows[h:h + 1, :], NEG)
                qk = lax.dot_general(q, k, (((1,), (1,)), ((), ())), preferred_element_type=F32)
                scores = qk * jnp.exp(dprime)
                inter = inter_all[:, h:h + 1]
                qc = jnp.dot(q, c_ref[s].astype(BF16), preferred_element_type=F32)
                num = jnp.dot(scores.astype(BF16), v, preferred_element_type=F32) + inter * qc
                qn = jnp.sum(q.astype(F32) * n_ref[s, 0:1, :], axis=1, keepdims=True)
                den = jnp.sum(scores, axis=1, keepdims=True) + inter * qn
                hh = num / jnp.maximum(jnp.abs(den), floor_all[:, h:h + 1])
                hh = _rms(hh, nw_ref[h])
                gate = jax.nn.sigmoid(o_ref[h, b, pl.ds(r0, Lc), :].astype(F32))
                h_ref[h, b, pl.ds(r0, Lc), :] = (gate * hh).astype(BF16)

                decay = decay_all[:, h:h + 1]
                kw = k.astype(F32) * w_all[:, h:h + 1]
                dc = lax.dot_general(kw.astype(BF16), v, (((0,), (0,)), ((), ())), preferred_element_type=F32)
                c_ref[s] = decay * c_ref[s] + dc
                n_ref[s, 0:1, :] = decay * n_ref[s, 0:1, :] + jnp.sum(kw, axis=0, keepdims=True)
        return carry

    lax.fori_loop(0, ts // Lc, chunk, 0)


def _mlstm(mq, mk, mv, mo, gates, norm_w, batch, seq_len):
    H, T, D = mq.shape
    ts = MLSTM_TILE
    nb = MLSTM_BATCHES if batch % MLSTM_BATCHES == 0 else 1
    assert seq_len % ts == 0
    nt = seq_len // ts
    per_seq = lambda t: t.reshape(H, batch, seq_len, D)
    hspec = pl.BlockSpec((H, nb, ts, D), lambda b, j: (0, b, j, 0))
    out = pl.pallas_call(
        _mlstm_kernel,
        grid=(batch // nb, nt),
        in_specs=[hspec, hspec, hspec, hspec,
                  pl.BlockSpec((nb, ts, 2 * LANES), lambda b, j: (b, j, 0)),
                  pl.BlockSpec((H, 1, D), lambda b, j: (0, 0, 0))],
        out_specs=hspec,
        out_shape=jax.ShapeDtypeStruct((H, batch, seq_len, D), BF16),
        scratch_shapes=[pltpu.VMEM((nb * H, D, D), F32), pltpu.VMEM((nb * H, HALO, D), F32),
                        pltpu.VMEM((nb, HALO, LANES), F32)],
        compiler_params=pltpu.CompilerParams(
            dimension_semantics=("arbitrary", "arbitrary"), vmem_limit_bytes=VMEM_LIMIT),
        name="mlstm",
    )(per_seq(mq), per_seq(mk), per_seq(mv), per_seq(mo), gates.reshape(batch, seq_len, 2 * LANES),
      norm_w.reshape(H, 1, D))
    return out.reshape(H, T, D)


def _ffn_kernel(x_ref, att_ref, hm_ref, permt_ref, wout_ref, fnw_ref, wup_ref, cw_ref, cb_ref, wdown_ref, finw_ref,
                o_ref, us_ref, *, tiles_per_seq, final_norm):
    tm = x_ref.shape[0]
    fc = FFN_CHUNK
    d_ff = wdown_ref.shape[0]
    nf = d_ff // fc
    i = pl.program_id(0)

    att = jnp.concatenate(
        [jnp.concatenate([att_ref[p, :, LANES * r:LANES * (r + 1)] for r in range(MAX_DIL)], axis=0)
         for p in range(ATT_PAIRS)], axis=1)
    att = jnp.dot(permt_ref[...], att, preferred_element_type=F32).astype(BF16)
    y = jnp.concatenate([att] + [hm_ref[h] for h in range(MLSTM_HEADS)], axis=1)
    h1 = x_ref[...] + jnp.dot(y, wout_ref[...], preferred_element_type=F32)
    o_ref[...] = h1
    xn = _rms(h1, fnw_ref[...]).astype(BF16)

    @pl.when(i % tiles_per_seq == 0)
    def _():
        us_ref[0:HALO, :] = jnp.zeros((HALO, us_ref.shape[1]), F32)

    us_ref[HALO:HALO + tm, :] = jnp.dot(xn, wup_ref[...], preferred_element_type=F32)

    def conv(c0):
        u = cb_ref[:, c0:c0 + fc]
        for j in range(FFN_CONV):
            off = HALO - (FFN_CONV - 1) + j
            u = u + us_ref[off:off + tm, c0:c0 + fc] * cw_ref[j:j + 1, c0:c0 + fc]
        return u

    acts = []
    for c in range(nf):
        gate, val = conv(c * fc), conv(d_ff + c * fc)
        acts.append((gate / (1.0 + jnp.exp(-gate)) * val).astype(BF16))
    us_ref[0:HALO, :] = us_ref[tm:tm + HALO, :]
    o_ref[...] += jnp.dot(jnp.concatenate(acts, axis=1), wdown_ref[...], preferred_element_type=F32)
    if final_norm:
        o_ref[...] = _rms(o_ref[...], finw_ref[...])


def _out_ffn(x2d, att, hm, w_out, ffn_norm_w, w_up, conv_w, conv_b, w_down, final_w, seq_len, final_norm):
    T, D = x2d.shape
    tm = FFN_TILE
    assert T % tm == 0 and seq_len % tm == 0
    d_ff = w_down.shape[0]
    assert d_ff % FFN_CHUNK == 0
    row = lambda i: (i, 0)
    const2 = lambda i: (0, 0)
    blk3 = lambda i: (0, i, 0)
    once = dict(pipeline_mode=pl.Buffered(1))
    assert tm == PERM_TILE
    permt = jnp.asarray(_dilation_permutation(PERM_TILE).T, BF16)
    return pl.pallas_call(
        functools.partial(_ffn_kernel, tiles_per_seq=seq_len // tm, final_norm=final_norm),
        grid=(T // tm,),
        in_specs=[
            pl.BlockSpec((tm, D), row),
            pl.BlockSpec((ATT_PAIRS, tm // MAX_DIL, MAX_DIL * LANES), blk3),
            pl.BlockSpec((MLSTM_HEADS, tm, MLSTM_HEAD_DIM), blk3),
            pl.BlockSpec(permt.shape, const2),
            pl.BlockSpec(w_out.shape, const2, **once),
            pl.BlockSpec((1, D), const2),
            pl.BlockSpec(w_up.shape, const2, **once),
            pl.BlockSpec(conv_w.shape, const2),
            pl.BlockSpec(conv_b.shape, const2),
            pl.BlockSpec(w_down.shape, const2, **once),
            pl.BlockSpec((1, D), const2),
        ],
        out_specs=pl.BlockSpec((tm, D), row),
        out_shape=jax.ShapeDtypeStruct((T, D), F32),
        scratch_shapes=[pltpu.VMEM((HALO + tm, 2 * d_ff), F32)],
        compiler_params=pltpu.CompilerParams(
            dimension_semantics=("arbitrary",), vmem_limit_bytes=VMEM_LIMIT),
        name="out_proj_conv_ffn",
    )(x2d, att, hm, permt, w_out, ffn_norm_w, w_up, conv_w, conv_b, w_down, final_w)


def kernel(x, w_in, mlstm_conv_w, mlstm_conv_b, mlstm_i_bias, mlstm_f_bias, att_out_norm_w, mlstm_out_norm_w, w_out, mixer_norm_w, ffn_norm_w, w_ffn_up, ffn_conv_w, ffn_conv_b, w_ffn_down, final_norm_w):
    batch, seq_len, d_model = x.shape
    depth = w_in.shape[0]
    h = x.reshape(batch * seq_len, d_model)
    for layer in range(depth):
        w = w_in[layer]
        n_gate = 2 * MLSTM_HEADS
        w_pad = jnp.pad(w, ((0, 0), (0, LANES - n_gate))).astype(BF16)
        gate_b = jnp.pad(jnp.concatenate([mlstm_i_bias[layer], mlstm_f_bias[layer]]), (0, LANES - n_gate))[None, :]
        aq, ak, av, mq, mk, mv, mo, gates = _input_projection(
            h, mixer_norm_w[layer][None, :], w_pad, mlstm_conv_w[layer], mlstm_conv_b[layer][None, :],
            gate_b, seq_len)
        att = _dilated_attention(aq, ak, av, att_out_norm_w[layer], batch, seq_len)
        hm = _mlstm(mq, mk, mv, mo, gates, mlstm_out_norm_w[layer], batch, seq_len)
        h = _out_ffn(h, att, hm, w_out[layer].astype(BF16), ffn_norm_w[layer][None, :], w_ffn_up[layer].astype(BF16),
                     ffn_conv_w[layer], ffn_conv_b[layer][None, :], w_ffn_down[layer].astype(BF16),
                     final_norm_w[None, :], seq_len, final_norm=(layer == depth - 1))
    return h.reshape(batch, seq_len, d_model)
```

```python
import functools

import numpy as np
import jax
import jax.numpy as jnp
from jax import lax
from jax.experimental import pallas as pl
from jax.experimental.pallas import tpu as pltpu

F32 = jnp.float32
BF16 = jnp.bfloat16

EPS = 1e-6
LANES = 128
ATT_HEADS = 8
ATT_HEAD_DIM = 64
ATT_WIDTH = ATT_HEADS * ATT_HEAD_DIM
ATT_PAIRS = ATT_WIDTH // LANES
ATT_BLOCK = 128
DILATIONS = (16, 4, 1)
MAX_DIL = 16
ATT_TILE = ATT_BLOCK * MAX_DIL
MLSTM_HEADS = 4
MLSTM_HEAD_DIM = 128
MLSTM_WIDTH = MLSTM_HEADS * MLSTM_HEAD_DIM
MLSTM_CHUNK = 64
MLSTM_CONV = 4
FFN_CONV = 3
HALO = 8
NEG = -1e30
LOG2E = 1.4426950408889634
VMEM_LIMIT = 56 * 1024 * 1024

PROJ_TILE = 1024
PERM_TILE = 512
FFN_TILE = 512
FFN_CHUNK = 256
MLSTM_TILE = 512
MLSTM_BATCHES = 4


def _rms(x, w):
    return x * lax.rsqrt(jnp.mean(x * x, axis=-1, keepdims=True) + EPS) * w


def _dilation_permutation(tile):
    rows = tile // MAX_DIL
    n = np.arange(tile)
    p = np.zeros((tile, tile), np.float32)
    p[n, MAX_DIL * (n % rows) + n // rows] = 1.0
    return p


def _proj_kernel(x_ref, nw_ref, w_ref, perm_ref, cw_ref, cb_ref, gb_ref,
                 aq_ref, ak_ref, av_ref, mq_ref, mk_ref, mv_ref, mo_ref, g_ref,
                 cs_ref, *, tiles_per_seq):
    tm = x_ref.shape[0]
    i = pl.program_id(0)
    xn = _rms(x_ref[...], nw_ref[...]).astype(BF16)
    xp = jnp.concatenate(
        [jnp.dot(perm_ref[...], xn[PERM_TILE * g:PERM_TILE * (g + 1)], preferred_element_type=F32).astype(BF16)
         for g in range(tm // PERM_TILE)], axis=0)
    base = 3 * ATT_WIDTH

    def seg(lhs, c0, width):
        return lambda: jnp.dot(lhs, w_ref[:, c0:c0 + width], preferred_element_type=F32)

    def put_blocks(ref, y):
        for p in range(ref.shape[0]):
            ref[p] = y[:, LANES * p:LANES * (p + 1)].astype(BF16)

    def put_view(ref, y):
        rows = PERM_TILE // MAX_DIL
        for g in range(tm // PERM_TILE):
            for p in range(ATT_PAIRS):
                for r in range(MAX_DIL):
                    src = PERM_TILE * g + rows * r
                    ref[p, rows * g:rows * (g + 1), LANES * r:LANES * (r + 1)] = (
                        y[src:src + rows, LANES * p:LANES * (p + 1)].astype(BF16))

    @pl.when(i % tiles_per_seq == 0)
    def _():
        cs_ref[0:HALO, :] = jnp.zeros((HALO, cs_ref.shape[1]), F32)

    def conv_silu(y):
        cs_ref[HALO:HALO + tm, :] = y
        y = cb_ref[...]
        for j in range(MLSTM_CONV):
            off = HALO - (MLSTM_CONV - 1) + j
            y = y + cs_ref[off:off + tm, :] * cw_ref[j:j + 1, :]
        cs_ref[0:HALO, :] = cs_ref[tm:tm + HALO, :]
        y = y / (1.0 + jnp.exp(-y))
        put_blocks(mq_ref, y[:, :MLSTM_WIDTH])
        put_blocks(mk_ref, y[:, MLSTM_WIDTH:] * (MLSTM_HEAD_DIM ** -0.5))

    def gates(z):
        z = z + gb_ref[...]
        g_ref[:, 0:LANES] = z
        log_sig = jnp.minimum(z, 0.0) - jnp.log1p(jnp.exp(-jnp.abs(z)))
        g_ref[:, LANES:2 * LANES] = pltpu.roll(log_sig, LANES - MLSTM_HEADS, axis=1)

    stages = [
        (seg(xp, 0, ATT_WIDTH), lambda y: put_view(aq_ref, y * (ATT_HEAD_DIM ** -0.5 * LOG2E))),
        (seg(xp, ATT_WIDTH, ATT_WIDTH), lambda y: put_view(ak_ref, y)),
        (seg(xp, 2 * ATT_WIDTH, ATT_WIDTH), lambda y: put_view(av_ref, y)),
        (seg(xn, base, 2 * MLSTM_WIDTH), conv_silu),
        (seg(xn, base + 2 * MLSTM_WIDTH, MLSTM_WIDTH), lambda y: put_blocks(mv_ref, y)),
        (seg(xn, base + 3 * MLSTM_WIDTH, MLSTM_WIDTH), lambda y: put_blocks(mo_ref, y)),
        (seg(xn, base + 4 * MLSTM_WIDTH, LANES), gates),
    ]
    y_next = stages[0][0]()
    for n, (_, epilogue) in enumerate(stages):
        y = y_next
        if n + 1 < len(stages):
            y_next = stages[n + 1][0]()
        epilogue(y)


def _input_projection(x2d, norm_w, w_pad, conv_w, conv_b, gate_b, seq_len):
    T, D = x2d.shape
    tm = PROJ_TILE
    assert T % tm == 0 and seq_len % tm == 0 and tm % PERM_TILE == 0
    row = lambda i: (i, 0)
    const2 = lambda i: (0, 0)
    blk3 = lambda i: (0, i, 0)
    perm = jnp.asarray(_dilation_permutation(PERM_TILE), BF16)
    heads_shape = jax.ShapeDtypeStruct((MLSTM_HEADS, T, LANES), BF16)
    heads_spec = pl.BlockSpec((MLSTM_HEADS, tm, LANES), blk3)
    view_shape = jax.ShapeDtypeStruct((ATT_PAIRS, T // MAX_DIL, MAX_DIL * LANES), BF16)
    view_spec = pl.BlockSpec((ATT_PAIRS, tm // MAX_DIL, MAX_DIL * LANES), blk3)
    return pl.pallas_call(
        functools.partial(_proj_kernel, tiles_per_seq=seq_len // tm),
        grid=(T // tm,),
        in_specs=[
            pl.BlockSpec((tm, D), row),
            pl.BlockSpec((1, D), const2),
            pl.BlockSpec(w_pad.shape, const2),
            pl.BlockSpec(perm.shape, const2),
            pl.BlockSpec(conv_w.shape, const2),
            pl.BlockSpec(conv_b.shape, const2),
            pl.BlockSpec(gate_b.shape, const2),
        ],
        out_specs=[view_spec] * 3 + [heads_spec] * 4 + [pl.BlockSpec((tm, 2 * LANES), row)],
        out_shape=[view_shape] * 3 + [heads_shape] * 4 + [jax.ShapeDtypeStruct((T, 2 * LANES), F32)],
        scratch_shapes=[pltpu.VMEM((HALO + tm, 2 * MLSTM_WIDTH), F32)],
        compiler_params=pltpu.CompilerParams(
            dimension_semantics=("arbitrary",), vmem_limit_bytes=VMEM_LIMIT),
        name="input_projection",
    )(x2d, norm_w, w_pad, perm, conv_w, conv_b, gate_b)


def _attention_biases():
    n = np.arange(ATT_BLOCK)
    s = np.arange(2 * ATT_BLOCK)
    out = np.zeros((3, 2, 2 * ATT_BLOCK, 2 * ATT_BLOCK), np.float32)
    pq16, pk16 = n, s
    pq4 = 4 * (n % 32) + n // 32
    pk4 = 4 * (s % 64) + s // 64
    perm1 = 16 * (n % 8) + n // 8
    pq1 = perm1
    pk1 = perm1[s % ATT_BLOCK] + ATT_BLOCK * (s // ATT_BLOCK)
    shifts = (ATT_BLOCK, 4 * 32, ATT_BLOCK)
    for g, (pq, pk, shift) in enumerate(((pq16, pk16, shifts[0]), (pq4, pk4, shifts[1]), (pq1, pk1, shifts[2]))):
        for variant, sh in enumerate((shift, 0)):
            dist = pq[:, None] - (pk[None, :] - sh)
            ok = (dist >= 0) & (dist <= ATT_BLOCK)
            bias = np.where(ok, 0.0, NEG).astype(np.float32)
            out[g, variant] = np.concatenate([bias, bias], axis=0)
    return out


def _attn_unit(q, kk, vv, bias, half0):
    zero = jnp.zeros_like(q)
    q2 = jnp.concatenate([jnp.where(half0, q, zero), jnp.where(half0, zero, q)], axis=0)
    s = lax.dot_general(q2, kk, (((1,), (1,)), ((), ())), preferred_element_type=F32) + bias
    m = jnp.max(s, axis=-1, keepdims=True)
    p = jnp.exp2(s - m).astype(BF16)
    vext = jnp.concatenate([vv, jnp.ones_like(vv)], axis=1)
    res = jnp.dot(p, vext, preferred_element_type=F32)
    top, bot = res[:ATT_BLOCK], res[ATT_BLOCK:]
    o = jnp.where(half0, top[:, :LANES], bot[:, :LANES])
    den = jnp.where(half0, top[:, LANES:], bot[:, LANES:])
    mb = jnp.where(half0, m[:ATT_BLOCK], m[ATT_BLOCK:])
    return o, mb, den


def _attention_kernel(q_ref, k_ref, v_ref, bias_ref, nw_ref, o_ref, qf_ref, kf_ref, vf_ref, st_ref):
    jt = pl.program_id(2)
    half0 = lax.broadcasted_iota(jnp.int32, (ATT_BLOCK, LANES), 1) < ATT_HEAD_DIM
    row0 = pl.multiple_of(jt * ATT_BLOCK, ATT_BLOCK)

    def lanes(r):
        return slice(LANES * r, LANES * (r + 1))

    @pl.when(jt == 0)
    def _():
        qf_ref[...] = q_ref[...].astype(F32)
        kf_ref[...] = k_ref[...].astype(F32)
        vf_ref[...] = v_ref[...].astype(F32)

    def store_stats(g, r, rows, stats):
        for st, val in enumerate(stats):
            st_ref[g, st, r, rows, :] = val

    first16 = jt == 0
    k0 = pl.multiple_of(jnp.maximum(row0 - ATT_BLOCK, 0), ATT_BLOCK)
    bias16 = bias_ref[0, jnp.where(first16, 1, 0)]
    for r in range(MAX_DIL):
        q = q_ref[pl.ds(row0, ATT_BLOCK), lanes(r)]
        kk = k_ref[pl.ds(k0, 2 * ATT_BLOCK), lanes(r)]
        vv = v_ref[pl.ds(k0, 2 * ATT_BLOCK), lanes(r)]
        store_stats(0, r, slice(None), _attn_unit(q, kk, vv, bias16, half0))

    def body4(bb, carry):
        gb = jt * 4 + bb
        q0 = pl.multiple_of(gb * 32, 32)
        k0 = pl.multiple_of(jnp.maximum(gb * 32 - 32, 0), 32)
        bias = bias_ref[1, jnp.where(gb == 0, 1, 0)]
        o0 = pl.multiple_of(bb * 32, 32)
        for r0 in range(4):
            q = jnp.concatenate([q_ref[pl.ds(q0, 32), lanes(4 * r1 + r0)] for r1 in range(4)], axis=0)
            kk = jnp.concatenate([k_ref[pl.ds(k0, 64), lanes(4 * r1 + r0)] for r1 in range(4)], axis=0)
            vv = jnp.concatenate([v_ref[pl.ds(k0, 64), lanes(4 * r1 + r0)] for r1 in range(4)], axis=0)
            stats = _attn_unit(q, kk, vv, bias, half0)
            for r1 in range(4):
                store_stats(1, 4 * r1 + r0, pl.ds(o0, 32), [s[32 * r1:32 * (r1 + 1)] for s in stats])
        return carry

    lax.fori_loop(0, 4, body4, 0, unroll=True)

    def body1(bl, carry):
        b = jt * MAX_DIL + bl
        q0 = pl.multiple_of(b * 8, 8)
        k0 = pl.multiple_of(jnp.maximum(b * 8 - 8, 0), 8)
        bias = bias_ref[2, jnp.where(b == 0, 1, 0)]
        o0 = pl.multiple_of(bl * 8, 8)

        def gather(ref, start):
            return jnp.concatenate([ref[pl.ds(start, 8), lanes(r)] for r in range(MAX_DIL)], axis=0)

        q = gather(qf_ref, q0).astype(BF16)
        kk = jnp.concatenate([gather(kf_ref, k0), gather(kf_ref, k0 + 8)], axis=0).astype(BF16)
        vv = jnp.concatenate([gather(vf_ref, k0), gather(vf_ref, k0 + 8)], axis=0).astype(BF16)
        stats = _attn_unit(q, kk, vv, bias, half0)
        for r in range(MAX_DIL):
            store_stats(2, r, pl.ds(o0, 8), [s[8 * r:8 * (r + 1)] for s in stats])
        return carry

    lax.fori_loop(0, MAX_DIL, body1, 0, unroll=True)

    nw = nw_ref[...]
    for r in range(MAX_DIL):
        ms = [st_ref[g, 1, r] for g in range(3)]
        mx = jnp.maximum(jnp.maximum(ms[0], ms[1]), ms[2])
        fs = [jnp.exp2(m - mx) for m in ms]
        num = fs[0] * st_ref[0, 0, r] + fs[1] * st_ref[1, 0, r] + fs[2] * st_ref[2, 0, r]
        den = fs[0] * st_ref[0, 2, r] + fs[1] * st_ref[1, 2, r] + fs[2] * st_ref[2, 2, r]
        att = num / den
        a2 = att * att
        s0 = jnp.sum(jnp.where(half0, a2, 0.0), axis=-1, keepdims=True)
        s1 = jnp.sum(jnp.where(half0, 0.0, a2), axis=-1, keepdims=True)
        msq = jnp.where(half0, s0, s1) * (1.0 / ATT_HEAD_DIM)
        o_ref[pl.ds(row0, ATT_BLOCK), lanes(r)] = (att * lax.rsqrt(msq + EPS) * nw).astype(BF16)


def _dilated_attention(aq, ak, av, norm_w, batch, seq_len):
    assert seq_len % ATT_TILE == 0 and seq_len >= 2 * ATT_TILE
    rows = seq_len // MAX_DIL
    nt = seq_len // ATT_TILE
    view = lambda t: t.reshape(ATT_PAIRS, batch, rows, MAX_DIL * LANES)
    bias = jnp.asarray(_attention_biases())
    seq_spec = pl.BlockSpec((None, None, rows, MAX_DIL * LANES), lambda b, p, j: (p, b, 0, 0))
    out = pl.pallas_call(
        _attention_kernel,
        grid=(batch, ATT_PAIRS, nt),
        in_specs=[seq_spec, seq_spec, seq_spec,
                  pl.BlockSpec(bias.shape, lambda b, p, j: (0, 0, 0, 0)),
                  pl.BlockSpec((None, 1, LANES), lambda b, p, j: (p, 0, 0))],
        out_specs=seq_spec,
        out_shape=jax.ShapeDtypeStruct((ATT_PAIRS, batch, rows, MAX_DIL * LANES), BF16),
        scratch_shapes=[pltpu.VMEM((rows, MAX_DIL * LANES), F32)] * 3
        + [pltpu.VMEM((3, 3, MAX_DIL, ATT_BLOCK, LANES), F32)],
        compiler_params=pltpu.CompilerParams(
            dimension_semantics=("arbitrary", "arbitrary", "arbitrary"), vmem_limit_bytes=VMEM_LIMIT),
        name="dilated_attention",
    )(view(aq), view(ak), view(av), bias, norm_w.reshape(ATT_PAIRS, 1, LANES))
    return out.reshape(ATT_PAIRS, batch * rows, MAX_DIL * LANES)


def _scan_sublanes(x, op, fill):
    n = x.shape[0]
    row = lax.broadcasted_iota(jnp.int32, x.shape, 0)
    sh = 1
    while sh < n:
        x = op(x, jnp.where(row >= sh, pltpu.roll(x, sh, axis=0), fill))
        sh *= 2
    return x


def _mlstm_kernel(q_ref, k_ref, v_ref, o_ref, g_ref, nw_ref, h_ref, c_ref, n_ref, m_ref):
    Lc, D, H = MLSTM_CHUNK, MLSTM_HEAD_DIM, MLSTM_HEADS
    nb, ts = q_ref.shape[1], q_ref.shape[2]

    @pl.when(pl.program_id(1) == 0)
    def _():
        c_ref[...] = jnp.zeros(c_ref.shape, F32)
        n_ref[...] = jnp.zeros(n_ref.shape, F32)
        m_ref[...] = jnp.zeros(m_ref.shape, F32)

    ti = lax.broadcasted_iota(jnp.int32, (Lc, Lc), 0)
    si = lax.broadcasted_iota(jnp.int32, (Lc, Lc), 1)
    causal = si <= ti

    def chunk(c, carry):
        r0 = pl.multiple_of(c * Lc, Lc)
        for b in range(nb):
            log_i = g_ref[b, pl.ds(r0, Lc), 0:LANES]
            log_f = g_ref[b, pl.ds(r0, Lc), LANES:2 * LANES]
            m_prev = m_ref[b, 0:1, :]
            bcum = _scan_sublanes(log_f, jnp.add, 0.0)
            g = bcum[Lc - 1:Lc, :]
            e = log_i - bcum
            m_t = bcum + jnp.maximum(m_prev, _scan_sublanes(e, jnp.maximum, NEG))
            c_minus_m = bcum - m_t
            inter_all = jnp.exp(c_minus_m + m_prev)
            floor_all = jnp.exp(-m_t)
            a = g + e
            m_new = jnp.maximum(g + m_prev, jnp.max(a, axis=0, keepdims=True))
            decay_all = jnp.exp(g + m_prev - m_new)
            w_all = jnp.exp(a - m_new)
            e_rows = jnp.transpose(e)
            m_ref[b, 0:1, :] = m_new
            for h in range(H):
                s = b * H + h
                q = q_ref[h, b, pl.ds(r0, Lc), :]
                k = k_ref[h, b, pl.ds(r0, Lc), :]
                v = v_ref[h, b, pl.ds(r0, Lc), :]
                dprime = jnp.where(causal, c_minus_m[:, h:h + 1] + e_rows[h:h + 1, :], NEG)
                qk = lax.dot_general(q, k, (((1,), (1,)), ((), ())), preferred_element_type=F32)
                scores = qk * jnp.exp(dprime)
                inter = inter_all[:, h:h + 1]
                qc = jnp.dot(q, c_ref[s].astype(BF16), preferred_element_type=F32)
                num = jnp.dot(scores.astype(BF16), v, preferred_element_type=F32) + inter * qc
                qn = jnp.sum(q.astype(F32) * n_ref[s, 0:1, :], axis=1, keepdims=True)
                den = jnp.sum(scores, axis=1, keepdims=True) + inter * qn
                hh = num / jnp.maximum(jnp.abs(den), floor_all[:, h:h + 1])
                hh = _rms(hh, nw_ref[h])
                gate = jax.nn.sigmoid(o_ref[h, b, pl.ds(r0, Lc), :].astype(F32))
                h_ref[h, b, pl.ds(r0, Lc), :] = (gate * hh).astype(BF16)

                decay = decay_all[:, h:h + 1]
                kw = k.astype(F32) * w_all[:, h:h + 1]
                dc = lax.dot_general(kw.astype(BF16), v, (((0,), (0,)), ((), ())), preferred_element_type=F32)
                c_ref[s] = decay * c_ref[s] + dc
                n_ref[s, 0:1, :] = decay * n_ref[s, 0:1, :] + jnp.sum(kw, axis=0, keepdims=True)
        return carry

    lax.fori_loop(0, ts // Lc, chunk, 0)


def _mlstm(mq, mk, mv, mo, gates, norm_w, batch, seq_len):
    H, T, D = mq.shape
    ts = MLSTM_TILE
    nb = MLSTM_BATCHES if batch % MLSTM_BATCHES == 0 else 1
    assert seq_len % ts == 0
    nt = seq_len // ts
    per_seq = lambda t: t.reshape(H, batch, seq_len, D)
    hspec = pl.BlockSpec((H, nb, ts, D), lambda b, j: (0, b, j, 0))
    out = pl.pallas_call(
        _mlstm_kernel,
        grid=(batch // nb, nt),
        in_specs=[hspec, hspec, hspec, hspec,
                  pl.BlockSpec((nb, ts, 2 * LANES), lambda b, j: (b, j, 0)),
                  pl.BlockSpec((H, 1, D), lambda b, j: (0, 0, 0))],
        out_specs=hspec,
        out_shape=jax.ShapeDtypeStruct((H, batch, seq_len, D), BF16),
        scratch_shapes=[pltpu.VMEM((nb * H, D, D), F32), pltpu.VMEM((nb * H, HALO, D), F32),
                        pltpu.VMEM((nb, HALO, LANES), F32)],
        compiler_params=pltpu.CompilerParams(
            dimension_semantics=("arbitrary", "arbitrary"), vmem_limit_bytes=VMEM_LIMIT),
        name="mlstm",
    )(per_seq(mq), per_seq(mk), per_seq(mv), per_seq(mo), gates.reshape(batch, seq_len, 2 * LANES),
      norm_w.reshape(H, 1, D))
    return out.reshape(H, T, D)


def _ffn_kernel(x_ref, att_ref, hm_ref, permt_ref, wout_ref, fnw_ref, wup_ref, cw_ref, cb_ref, wdown_ref, finw_ref,
                o_ref, us_ref, *, tiles_per_seq, final_norm):
    tm = x_ref.shape[0]
    fc = FFN_CHUNK
    d_ff = wdown_ref.shape[0]
    nf = d_ff // fc
    i = pl.program_id(0)

    att = jnp.concatenate(
        [jnp.concatenate([att_ref[p, :, LANES * r:LANES * (r + 1)] for r in range(MAX_DIL)], axis=0)
         for p in range(ATT_PAIRS)], axis=1)
    att = jnp.dot(permt_ref[...], att, preferred_element_type=F32).astype(BF16)
    y = jnp.concatenate([att] + [hm_ref[h] for h in range(MLSTM_HEADS)], axis=1)
    h1 = x_ref[...] + jnp.dot(y, wout_ref[...], preferred_element_type=F32)
    o_ref[...] = h1
    xn = _rms(h1, fnw_ref[...]).astype(BF16)

    @pl.when(i % tiles_per_seq == 0)
    def _():
        us_ref[0:HALO, :] = jnp.zeros((HALO, us_ref.shape[1]), F32)

    us_ref[HALO:HALO + tm, :] = jnp.dot(xn, wup_ref[...], preferred_element_type=F32)

    def conv(c0):
        u = cb_ref[:, c0:c0 + fc]
        for j in range(FFN_CONV):
            off = HALO - (FFN_CONV - 1) + j
            u = u + us_ref[off:off + tm, c0:c0 + fc] * cw_ref[j:j + 1, c0:c0 + fc]
        return u

    acts = []
    for c in range(nf):
        gate, val = conv(c * fc), conv(d_ff + c * fc)
        acts.append((gate / (1.0 + jnp.exp(-gate)) * val).astype(BF16))
    us_ref[0:HALO, :] = us_ref[tm:tm + HALO, :]
    o_ref[...] += jnp.dot(jnp.concatenate(acts, axis=1), wdown_ref[...], preferred_element_type=F32)
    if final_norm:
        o_ref[...] = _rms(o_ref[...], finw_ref[...])


def _out_ffn(x2d, att, hm, w_out, ffn_norm_w, w_up, conv_w, conv_b, w_down, final_w, seq_len, final_norm):
    T, D = x2d.shape
    tm = FFN_TILE
    assert T % tm == 0 and seq_len % tm == 0
    d_ff = w_down.shape[0]
    assert d_ff % FFN_CHUNK == 0
    row = lambda i: (i, 0)
    const2 = lambda i: (0, 0)
    blk3 = lambda i: (0, i, 0)
    once = dict(pipeline_mode=pl.Buffered(1))
    assert tm == PERM_TILE
    permt = jnp.asarray(_dilation_permutation(PERM_TILE).T, BF16)
    return pl.pallas_call(
        functools.partial(_ffn_kernel, tiles_per_seq=seq_len // tm, final_norm=final_norm),
        grid=(T // tm,),
        in_specs=[
            pl.BlockSpec((tm, D), row),
            pl.BlockSpec((ATT_PAIRS, tm // MAX_DIL, MAX_DIL * LANES), blk3),
            pl.BlockSpec((MLSTM_HEADS, tm, MLSTM_HEAD_DIM), blk3),
            pl.BlockSpec(permt.shape, const2),
            pl.BlockSpec(w_out.shape, const2, **once),
            pl.BlockSpec((1, D), const2),
            pl.BlockSpec(w_up.shape, const2, **once),
            pl.BlockSpec(conv_w.shape, const2),
            pl.BlockSpec(conv_b.shape, const2),
            pl.BlockSpec(w_down.shape, const2, **once),
            pl.BlockSpec((1, D), const2),
        ],
        out_specs=pl.BlockSpec((tm, D), row),
        out_shape=jax.ShapeDtypeStruct((T, D), F32),
        scratch_shapes=[pltpu.VMEM((HALO + tm, 2 * d_ff), F32)],
        compiler_params=pltpu.CompilerParams(
            dimension_semantics=("arbitrary",), vmem_limit_bytes=VMEM_LIMIT),
        name="out_proj_conv_ffn",
    )(x2d, att, hm, permt, w_out, ffn_norm_w, w_up, conv_w, conv_b, w_down, final_w)


def kernel(x, w_in, mlstm_conv_w, mlstm_conv_b, mlstm_i_bias, mlstm_f_bias, att_out_norm_w, mlstm_out_norm_w, w_out, mixer_norm_w, ffn_norm_w, w_ffn_up, ffn_conv_w, ffn_conv_b, w_ffn_down, final_norm_w):
    batch, seq_len, d_model = x.shape
    depth = w_in.shape[0]
    h = x.reshape(batch * seq_len, d_model)
    for layer in range(depth):
        w = w_in[layer]
        n_gate = 2 * MLSTM_HEADS
        w_pad = jnp.pad(w, ((0, 0), (0, LANES - n_gate))).astype(BF16)
        gate_b = jnp.pad(jnp.concatenate([mlstm_i_bias[layer], mlstm_f_bias[layer]]), (0, LANES - n_gate))[None, :]
        aq, ak, av, mq, mk, mv, mo, gates = _input_projection(
            h, mixer_norm_w[layer][None, :], w_pad, mlstm_conv_w[layer], mlstm_conv_b[layer][None, :],
            gate_b, seq_len)
        att = _dilated_attention(aq, ak, av, att_out_norm_w[layer], batch, seq_len)
        hm = _mlstm(mq, mk, mv, mo, gates, mlstm_out_norm_w[layer], batch, seq_len)
        h = _out_ffn(h, att, hm, w_out[layer].astype(BF16), ffn_norm_w[layer][None, :], w_ffn_up[layer].astype(BF16),
                     ffn_conv_w[layer], ffn_conv_b[layer][None, :], w_ffn_down[layer].astype(BF16),
                     final_norm_w[None, :], seq_len, final_norm=(layer == depth - 1))
    return h.reshape(batch, seq_len, d_model)
```

```python
import functools

import numpy as np
import jax
import jax.numpy as jnp
from jax import lax
from jax.experimental import pallas as pl
from jax.experimental.pallas import tpu as pltpu

F32 = jnp.float32
BF16 = jnp.bfloat16

EPS = 1e-6
LANES = 128
ATT_HEADS = 8
ATT_HEAD_DIM = 64
ATT_WIDTH = ATT_HEADS * ATT_HEAD_DIM
ATT_PAIRS = ATT_WIDTH // LANES
ATT_BLOCK = 128
DILATIONS = (16, 4, 1)
MAX_DIL = 16
ATT_TILE = ATT_BLOCK * MAX_DIL
MLSTM_HEADS = 4
MLSTM_HEAD_DIM = 128
MLSTM_WIDTH = MLSTM_HEADS * MLSTM_HEAD_DIM
MLSTM_KERNEL_CHUNK = 128
MLSTM_CONV = 4
FFN_CONV = 3
SUBLANES = 8
HALO = SUBLANES
NEG = -1e30
LOG2E = 1.4426950408889634
VMEM_LIMIT = 56 * 1024 * 1024

PROJ_TILE = 1024
PERM_TILE = 512
FFN_TILE = 512
FFN_CHUNK = 256
MLSTM_TILE = 512
MLSTM_BATCHES = 4


def _rms(x, w):
    return x * lax.rsqrt(jnp.mean(x * x, axis=-1, keepdims=True) + EPS) * w


def _dilation_permutation(tile):
    rows = tile // MAX_DIL
    n = np.arange(tile)
    p = np.zeros((tile, tile), np.float32)
    p[n, MAX_DIL * (n % rows) + n // rows] = 1.0
    return p


def _proj_kernel(x_ref, nw_ref, w_ref, wt_ref, perm_ref, cw_ref, cb_ref, gb_ref,
                 aq_ref, ak_ref, av_ref, mq_ref, mk_ref, mv_ref, mo_ref, g_ref,
                 cs_ref, *, tiles_per_seq):
    tm = x_ref.shape[0]
    i = pl.program_id(0)
    xn = _rms(x_ref[...], nw_ref[...]).astype(BF16)
    xp = jnp.concatenate(
        [jnp.dot(perm_ref[...], xn[PERM_TILE * g:PERM_TILE * (g + 1)], preferred_element_type=F32).astype(BF16)
         for g in range(tm // PERM_TILE)], axis=0)
    base = 3 * ATT_WIDTH

    def seg(lhs, c0, width):
        return lambda: jnp.dot(lhs, w_ref[:, c0:c0 + width], preferred_element_type=F32)

    def put_blocks(ref, y):
        for p in range(ref.shape[0]):
            ref[p] = y[:, LANES * p:LANES * (p + 1)].astype(BF16)

    def put_view(ref, y):
        rows = PERM_TILE // MAX_DIL
        for g in range(tm // PERM_TILE):
            for p in range(ATT_PAIRS):
                for r in range(MAX_DIL):
                    src = PERM_TILE * g + rows * r
                    ref[p, rows * g:rows * (g + 1), LANES * r:LANES * (r + 1)] = (
                        y[src:src + rows, LANES * p:LANES * (p + 1)].astype(BF16))

    @pl.when(i % tiles_per_seq == 0)
    def _():
        cs_ref[0:HALO, :] = jnp.zeros((HALO, cs_ref.shape[1]), F32)

    def put_transposed(ref, y):
        for p in range(ref.shape[0]):
            ref[p] = jnp.transpose(y[:, LANES * p:LANES * (p + 1)]).astype(BF16)

    def conv_silu(y):
        cs_ref[HALO:HALO + tm, :] = y
        y = cb_ref[...]
        for j in range(MLSTM_CONV):
            off = HALO - (MLSTM_CONV - 1) + j
            y = y + cs_ref[off:off + tm, :] * cw_ref[j:j + 1, :]
        cs_ref[0:HALO, :] = cs_ref[tm:tm + HALO, :]
        y = y / (1.0 + jnp.exp(-y))
        put_transposed(mq_ref, y[:, :MLSTM_WIDTH])
        put_blocks(mk_ref, y[:, MLSTM_WIDTH:] * (MLSTM_HEAD_DIM ** -0.5))

    def seg_t(r0, rows):
        return lambda: lax.dot_general(wt_ref[r0:r0 + rows, :], xn, (((1,), (1,)), ((), ())),
                                       preferred_element_type=F32)

    def put_rows(ref):
        def f(y):
            for p in range(ref.shape[0]):
                ref[p] = y[LANES * p:LANES * (p + 1), :].astype(BF16)
        return f

    def gates(z):
        z = z + gb_ref[...]
        g_ref[0] = z[0:SUBLANES]
        zf = z[SUBLANES:2 * SUBLANES]
        g_ref[1] = jnp.minimum(zf, 0.0) - jnp.log1p(jnp.exp(-jnp.abs(zf)))

    stages = [
        (seg(xp, 0, ATT_WIDTH), lambda y: put_view(aq_ref, y * (ATT_HEAD_DIM ** -0.5 * LOG2E))),
        (seg(xp, ATT_WIDTH, ATT_WIDTH), lambda y: put_view(ak_ref, y)),
        (seg(xp, 2 * ATT_WIDTH, ATT_WIDTH), lambda y: put_view(av_ref, y)),
        (seg(xn, base, 2 * MLSTM_WIDTH), conv_silu),
        (seg_t(0, MLSTM_WIDTH), put_rows(mv_ref)),
        (seg_t(MLSTM_WIDTH, MLSTM_WIDTH), put_rows(mo_ref)),
        (seg_t(2 * MLSTM_WIDTH, 2 * SUBLANES), gates),
    ]
    y_next = stages[0][0]()
    for n, (_, epilogue) in enumerate(stages):
        y = y_next
        if n + 1 < len(stages):
            y_next = stages[n + 1][0]()
        epilogue(y)


def _input_projection(x2d, norm_w, w_nat, w_t, conv_w, conv_b, gate_b, batch, seq_len):
    T, D = x2d.shape
    tm = PROJ_TILE
    assert T % tm == 0 and seq_len % tm == 0 and tm % PERM_TILE == 0
    tps = seq_len // tm
    row = lambda i: (i, 0)
    const2 = lambda i: (0, 0)
    blk3 = lambda i: (0, i, 0)
    perm = jnp.asarray(_dilation_permutation(PERM_TILE), BF16)
    H, Dh = MLSTM_HEADS, MLSTM_HEAD_DIM
    heads_shape = jax.ShapeDtypeStruct((H, T, Dh), BF16)
    heads_spec = pl.BlockSpec((H, tm, Dh), blk3)
    heads_t_shape = jax.ShapeDtypeStruct((H, batch, Dh, seq_len), BF16)
    heads_t_spec = pl.BlockSpec((H, None, Dh, tm), lambda i: (0, i // tps, 0, i % tps))
    view_shape = jax.ShapeDtypeStruct((ATT_PAIRS, T // MAX_DIL, MAX_DIL * LANES), BF16)
    view_spec = pl.BlockSpec((ATT_PAIRS, tm // MAX_DIL, MAX_DIL * LANES), blk3)
    gate_shape = jax.ShapeDtypeStruct((batch, 2, SUBLANES, seq_len), F32)
    gate_spec = pl.BlockSpec((None, 2, SUBLANES, tm), lambda i: (i // tps, 0, 0, i % tps))
    return pl.pallas_call(
        functools.partial(_proj_kernel, tiles_per_seq=tps),
        grid=(T // tm,),
        in_specs=[
            pl.BlockSpec((tm, D), row),
            pl.BlockSpec((1, D), const2),
            pl.BlockSpec(w_nat.shape, const2),
            pl.BlockSpec(w_t.shape, const2),
            pl.BlockSpec(perm.shape, const2),
            pl.BlockSpec(conv_w.shape, const2),
            pl.BlockSpec(conv_b.shape, const2),
            pl.BlockSpec(gate_b.shape, const2),
        ],
        out_specs=[view_spec] * 3 + [heads_t_spec, heads_spec, heads_t_spec, heads_t_spec, gate_spec],
        out_shape=[view_shape] * 3 + [heads_t_shape, heads_shape, heads_t_shape, heads_t_shape, gate_shape],
        scratch_shapes=[pltpu.VMEM((HALO + tm, 2 * MLSTM_WIDTH), F32)],
        compiler_params=pltpu.CompilerParams(
            dimension_semantics=("arbitrary",), vmem_limit_bytes=VMEM_LIMIT),
        name="input_projection",
    )(x2d, norm_w, w_nat, w_t, perm, conv_w, conv_b, gate_b)


def _attention_biases():
    n = np.arange(ATT_BLOCK)
    s = np.arange(2 * ATT_BLOCK)
    out = np.zeros((3, 2, 2 * ATT_BLOCK, 2 * ATT_BLOCK), np.float32)
    pq16, pk16 = n, s
    pq4 = 4 * (n % 32) + n // 32
    pk4 = 4 * (s % 64) + s // 64
    perm1 = 16 * (n % 8) + n // 8
    pq1 = perm1
    pk1 = perm1[s % ATT_BLOCK] + ATT_BLOCK * (s // ATT_BLOCK)
    shifts = (ATT_BLOCK, 4 * 32, ATT_BLOCK)
    for g, (pq, pk, shift) in enumerate(((pq16, pk16, shifts[0]), (pq4, pk4, shifts[1]), (pq1, pk1, shifts[2]))):
        for variant, sh in enumerate((shift, 0)):
            dist = pq[:, None] - (pk[None, :] - sh)
            ok = (dist >= 0) & (dist <= ATT_BLOCK)
            bias = np.where(ok, 0.0, NEG).astype(np.float32)
            out[g, variant] = np.concatenate([bias, bias], axis=0)
    return out


def _attn_unit(q, kk, vv, bias, half0):
    zero = jnp.zeros_like(q)
    q2 = jnp.concatenate([jnp.where(half0, q, zero), jnp.where(half0, zero, q)], axis=0)
    s = lax.dot_general(q2, kk, (((1,), (1,)), ((), ())), preferred_element_type=F32) + bias
    m = jnp.max(s, axis=-1, keepdims=True)
    p = jnp.exp2(s - m).astype(BF16)
    vext = jnp.concatenate([vv, jnp.ones_like(vv)], axis=1)
    res = jnp.dot(p, vext, preferred_element_type=F32)
    top, bot = res[:ATT_BLOCK], res[ATT_BLOCK:]
    o = jnp.where(half0, top[:, :LANES], bot[:, :LANES])
    den = jnp.where(half0, top[:, LANES:], bot[:, LANES:])
    mb = jnp.where(half0, m[:ATT_BLOCK], m[ATT_BLOCK:])
    return o, mb, den


def _attention_kernel(q_ref, k_ref, v_ref, bias_ref, nw_ref, o_ref, qf_ref, kf_ref, vf_ref, st_ref):
    jt = pl.program_id(2)
    half0 = lax.broadcasted_iota(jnp.int32, (ATT_BLOCK, LANES), 1) < ATT_HEAD_DIM
    row0 = pl.multiple_of(jt * ATT_BLOCK, ATT_BLOCK)

    def lanes(r):
        return slice(LANES * r, LANES * (r + 1))

    @pl.when(jt == 0)
    def _():
        qf_ref[...] = q_ref[...].astype(F32)
        kf_ref[...] = k_ref[...].astype(F32)
        vf_ref[...] = v_ref[...].astype(F32)

    def store_stats(g, r, rows, stats):
        for st, val in enumerate(stats):
            st_ref[g, st, r, rows, :] = val

    first16 = jt == 0
    k0 = pl.multiple_of(jnp.maximum(row0 - ATT_BLOCK, 0), ATT_BLOCK)
    bias16 = bias_ref[0, jnp.where(first16, 1, 0)]
    for r in range(MAX_DIL):
        q = q_ref[pl.ds(row0, ATT_BLOCK), lanes(r)]
        kk = k_ref[pl.ds(k0, 2 * ATT_BLOCK), lanes(r)]
        vv = v_ref[pl.ds(k0, 2 * ATT_BLOCK), lanes(r)]
        store_stats(0, r, slice(None), _attn_unit(q, kk, vv, bias16, half0))

    def body4(bb, carry):
        gb = jt * 4 + bb
        q0 = pl.multiple_of(gb * 32, 32)
        k0 = pl.multiple_of(jnp.maximum(gb * 32 - 32, 0), 32)
        bias = bias_ref[1, jnp.where(gb == 0, 1, 0)]
        o0 = pl.multiple_of(bb * 32, 32)
        for r0 in range(4):
            q = jnp.concatenate([q_ref[pl.ds(q0, 32), lanes(4 * r1 + r0)] for r1 in range(4)], axis=0)
            kk = jnp.concatenate([k_ref[pl.ds(k0, 64), lanes(4 * r1 + r0)] for r1 in range(4)], axis=0)
            vv = jnp.concatenate([v_ref[pl.ds(k0, 64), lanes(4 * r1 + r0)] for r1 in range(4)], axis=0)
            stats = _attn_unit(q, kk, vv, bias, half0)
            for r1 in range(4):
                store_stats(1, 4 * r1 + r0, pl.ds(o0, 32), [s[32 * r1:32 * (r1 + 1)] for s in stats])
        return carry

    lax.fori_loop(0, 4, body4, 0, unroll=True)

    def body1(bl, carry):
        b = jt * MAX_DIL + bl
        q0 = pl.multiple_of(b * 8, 8)
        k0 = pl.multiple_of(jnp.maximum(b * 8 - 8, 0), 8)
        bias = bias_ref[2, jnp.where(b == 0, 1, 0)]
        o0 = pl.multiple_of(bl * 8, 8)

        def gather(ref, start):
            return jnp.concatenate([ref[pl.ds(start, 8), lanes(r)] for r in range(MAX_DIL)], axis=0)

        q = gather(qf_ref, q0).astype(BF16)
        kk = jnp.concatenate([gather(kf_ref, k0), gather(kf_ref, k0 + 8)], axis=0).astype(BF16)
        vv = jnp.concatenate([gather(vf_ref, k0), gather(vf_ref, k0 + 8)], axis=0).astype(BF16)
        stats = _attn_unit(q, kk, vv, bias, half0)
        for r in range(MAX_DIL):
            store_stats(2, r, pl.ds(o0, 8), [s[8 * r:8 * (r + 1)] for s in stats])
        return carry

    lax.fori_loop(0, MAX_DIL, body1, 0, unroll=True)

    nw = nw_ref[...]
    for r in range(MAX_DIL):
        ms = [st_ref[g, 1, r] for g in range(3)]
        mx = jnp.maximum(jnp.maximum(ms[0], ms[1]), ms[2])
        fs = [jnp.exp2(m - mx) for m in ms]
        num = fs[0] * st_ref[0, 0, r] + fs[1] * st_ref[1, 0, r] + fs[2] * st_ref[2, 0, r]
        den = fs[0] * st_ref[0, 2, r] + fs[1] * st_ref[1, 2, r] + fs[2] * st_ref[2, 2, r]
        att = num / den
        a2 = att * att
        s0 = jnp.sum(jnp.where(half0, a2, 0.0), axis=-1, keepdims=True)
        s1 = jnp.sum(jnp.where(half0, 0.0, a2), axis=-1, keepdims=True)
        msq = jnp.where(half0, s0, s1) * (1.0 / ATT_HEAD_DIM)
        o_ref[pl.ds(row0, ATT_BLOCK), lanes(r)] = (att * lax.rsqrt(msq + EPS) * nw).astype(BF16)


def _dilated_attention(aq, ak, av, norm_w, batch, seq_len):
    assert seq_len % ATT_TILE == 0 and seq_len >= 2 * ATT_TILE
    rows = seq_len // MAX_DIL
    nt = seq_len // ATT_TILE
    view = lambda t: t.reshape(ATT_PAIRS, batch, rows, MAX_DIL * LANES)
    bias = jnp.asarray(_attention_biases())
    seq_spec = pl.BlockSpec((None, None, rows, MAX_DIL * LANES), lambda b, p, j: (p, b, 0, 0))
    out = pl.pallas_call(
        _attention_kernel,
        grid=(batch, ATT_PAIRS, nt),
        in_specs=[seq_spec, seq_spec, seq_spec,
                  pl.BlockSpec(bias.shape, lambda b, p, j: (0, 0, 0, 0)),
                  pl.BlockSpec((None, 1, LANES), lambda b, p, j: (p, 0, 0))],
        out_specs=seq_spec,
        out_shape=jax.ShapeDtypeStruct((ATT_PAIRS, batch, rows, MAX_DIL * LANES), BF16),
        scratch_shapes=[pltpu.VMEM((rows, MAX_DIL * LANES), F32)] * 3
        + [pltpu.VMEM((3, 3, MAX_DIL, ATT_BLOCK, LANES), F32)],
        compiler_params=pltpu.CompilerParams(
            dimension_semantics=("arbitrary", "arbitrary", "arbitrary"), vmem_limit_bytes=VMEM_LIMIT),
        name="dilated_attention",
    )(view(aq), view(ak), view(av), bias, norm_w.reshape(ATT_PAIRS, 1, LANES))
    return out.reshape(ATT_PAIRS, batch * rows, MAX_DIL * LANES)


def _scan_lanes(x, op, fill):
    n = x.shape[1]
    lane = lax.broadcasted_iota(jnp.int32, x.shape, 1)
    sh = 1
    while sh < n:
        x = op(x, jnp.where(lane >= sh, pltpu.roll(x, sh, axis=1), fill))
        sh *= 2
    return x


def _two_term_rows(row):
    r = lax.broadcasted_iota(jnp.int32, (2 * SUBLANES, row.shape[1]), 0)
    full = jnp.broadcast_to(row, r.shape)
    hi = full.astype(BF16).astype(F32)
    return jnp.where(r == 0, hi, jnp.where(r == 1, full - hi, 0.0)).astype(BF16)


def _mlstm_kernel(q_ref, k_ref, v_ref, o_ref, g_ref, nw_ref, h_ref, c_ref, n_ref, m_ref, sc_ref, et_ref):
    Lc, D, H = MLSTM_KERNEL_CHUNK, MLSTM_HEAD_DIM, MLSTM_HEADS
    nb, ts = q_ref.shape[1], q_ref.shape[3]

    @pl.when(pl.program_id(1) == 0)
    def _():
        c_ref[...] = jnp.zeros(c_ref.shape, F32)
        n_ref[...] = jnp.zeros(n_ref.shape, F32)
        m_ref[...] = jnp.zeros(m_ref.shape, F32)

    si = lax.broadcasted_iota(jnp.int32, (Lc, Lc), 0)
    ti = lax.broadcasted_iota(jnp.int32, (Lc, Lc), 1)
    causal = si <= ti
    nc = ts // Lc

    tiles = [(c, b) for c in range(nc) for b in range(nb)]
    log_i = jnp.concatenate([g_ref[b, 0, :, Lc * c:Lc * (c + 1)] for c, b in tiles], axis=0)
    log_f = jnp.concatenate([g_ref[b, 1, :, Lc * c:Lc * (c + 1)] for c, b in tiles], axis=0)
    bcum = _scan_lanes(log_f, jnp.add, 0.0)
    e = log_i - bcum
    sc_ref[0] = bcum
    sc_ref[1] = e
    sc_ref[2] = _scan_lanes(e, jnp.maximum, NEG)
    rows_c = nb * SUBLANES
    for c in range(nc):
        e_c = e[rows_c * c:rows_c * (c + 1)]
        et_ref[c] = jnp.transpose(jnp.concatenate([e_c, jnp.zeros((Lc - rows_c, Lc), F32)], axis=0))

    def chunk(c, carry):
        t0 = pl.multiple_of(c * Lc, Lc)
        e_cols = et_ref[c]
        for b in range(nb):
            r0 = pl.multiple_of((c * nb + b) * SUBLANES, SUBLANES)
            bcum = sc_ref[0, pl.ds(r0, SUBLANES), :]
            e = sc_ref[1, pl.ds(r0, SUBLANES), :]
            m_prev = m_ref[b]
            g = bcum[:, Lc - 1:Lc]
            m_t = bcum + jnp.maximum(m_prev, sc_ref[2, pl.ds(r0, SUBLANES), :])
            c_minus_m = bcum - m_t
            inter_all = jnp.exp(c_minus_m + m_prev)
            floor_all = jnp.exp(-m_t)
            a = g + e
            m_new = jnp.maximum(g + m_prev, jnp.max(a, axis=1, keepdims=True))
            decay_all = jnp.exp(g + m_prev - m_new)
            w_all = jnp.exp(a - m_new)
            m_ref[b] = jnp.broadcast_to(m_new, m_prev.shape)
            for h in range(H):
                s = b * H + h
                qt = q_ref[h, b, :, pl.ds(t0, Lc)]
                k = k_ref[h, b, pl.ds(t0, Lc), :]
                vt = v_ref[h, b, :, pl.ds(t0, Lc)]
                col = SUBLANES * b + h
                dprime = jnp.where(causal, e_cols[:, col:col + 1] + c_minus_m[h:h + 1, :], NEG)
                scores = jnp.dot(k, qt, preferred_element_type=F32) * jnp.exp(dprime)
                inter = inter_all[h:h + 1, :]
                cq = jnp.dot(c_ref[s].astype(BF16), qt, preferred_element_type=F32)
                num = jnp.dot(vt, scores.astype(BF16), preferred_element_type=F32) + inter * cq
                nq = jnp.dot(_two_term_rows(n_ref[s, 0:1, :]), qt, preferred_element_type=F32)
                den = jnp.sum(scores, axis=0, keepdims=True) + inter * (nq[0:1, :] + nq[1:2, :])
                hh = num * (1.0 / jnp.maximum(jnp.abs(den), floor_all[h:h + 1, :]))
                hh = hh * lax.rsqrt(jnp.mean(hh * hh, axis=0, keepdims=True) + EPS) * nw_ref[h]
                gate = jax.nn.sigmoid(o_ref[h, b, :, pl.ds(t0, Lc)].astype(F32))
                h_ref[h, b, :, pl.ds(t0, Lc)] = (gate * hh).astype(BF16)

                decay = decay_all[h:h + 1, :]
                w_row = w_all[h:h + 1, :]
                dc = jnp.dot((vt.astype(F32) * w_row).astype(BF16), k, preferred_element_type=F32)
                c_ref[s] = decay * c_ref[s] + dc
                dn = jnp.dot(_two_term_rows(w_row), k, preferred_element_type=F32)
                n_ref[s, 0:1, :] = decay * n_ref[s, 0:1, :] + dn[0:1, :] + dn[1:2, :]
        return carry

    lax.fori_loop(0, ts // Lc, chunk, 0)


def _mlstm(mqt, mk, mvt, mot, gates, norm_w, batch, seq_len):
    H, D = MLSTM_HEADS, MLSTM_HEAD_DIM
    ts = MLSTM_TILE
    nb = MLSTM_BATCHES if batch % MLSTM_BATCHES == 0 else 1
    assert seq_len % ts == 0 and ts % MLSTM_KERNEL_CHUNK == 0 and D == MLSTM_KERNEL_CHUNK
    nt = seq_len // ts
    tspec = pl.BlockSpec((H, nb, D, ts), lambda b, j: (0, b, 0, j))
    nw_cols = jnp.broadcast_to(norm_w.reshape(H, D, 1), (H, D, LANES))
    return pl.pallas_call(
        _mlstm_kernel,
        grid=(batch // nb, nt),
        in_specs=[tspec,
                  pl.BlockSpec((H, nb, ts, D), lambda b, j: (0, b, j, 0)),
                  tspec, tspec,
                  pl.BlockSpec((nb, 2, SUBLANES, ts), lambda b, j: (b, 0, 0, j)),
                  pl.BlockSpec((H, D, LANES), lambda b, j: (0, 0, 0))],
        out_specs=tspec,
        out_shape=jax.ShapeDtypeStruct((H, batch, D, seq_len), BF16),
        scratch_shapes=[pltpu.VMEM((nb * H, D, D), F32), pltpu.VMEM((nb * H, HALO, D), F32),
                        pltpu.VMEM((nb, SUBLANES, LANES), F32),
                        pltpu.VMEM((3, (ts // D) * nb * SUBLANES, D), F32), pltpu.VMEM((ts // D, D, D), F32)],
        compiler_params=pltpu.CompilerParams(
            dimension_semantics=("arbitrary", "arbitrary"), vmem_limit_bytes=VMEM_LIMIT),
        name="mlstm",
    )(mqt, mk.reshape(H, batch, seq_len, D), mvt, mot, gates, nw_cols)


def _ffn_kernel(x_ref, att_ref, hm_ref, permt_ref, wout_ref, fnw_ref, wup_ref, cw_ref, cb_ref, wdown_ref, finw_ref,
                o_ref, us_ref, *, tiles_per_seq, final_norm):
    tm = x_ref.shape[0]
    fc = FFN_CHUNK
    d_ff = wdown_ref.shape[0]
    nf = d_ff // fc
    i = pl.program_id(0)

    att = jnp.concatenate(
        [jnp.concatenate([att_ref[p, :, LANES * r:LANES * (r + 1)] for r in range(MAX_DIL)], axis=0)
         for p in range(ATT_PAIRS)], axis=1)
    att = jnp.dot(permt_ref[...], att, preferred_element_type=F32).astype(BF16)
    hm = [jnp.transpose(hm_ref[h].astype(F32)).astype(BF16) for h in range(MLSTM_HEADS)]
    y = jnp.concatenate([att] + hm, axis=1)
    h1 = x_ref[...] + jnp.dot(y, wout_ref[...], preferred_element_type=F32)
    o_ref[...] = h1
    xn = _rms(h1, fnw_ref[...]).astype(BF16)

    @pl.when(i % tiles_per_seq == 0)
    def _():
        us_ref[0:HALO, :] = jnp.zeros((HALO, us_ref.shape[1]), F32)

    us_ref[HALO:HALO + tm, :] = jnp.dot(xn, wup_ref[...], preferred_element_type=F32)

    def conv(c0):
        u = cb_ref[:, c0:c0 + fc]
        for j in range(FFN_CONV):
            off = HALO - (FFN_CONV - 1) + j
            u = u + us_ref[off:off + tm, c0:c0 + fc] * cw_ref[j:j + 1, c0:c0 + fc]
        return u

    acts = []
    for c in range(nf):
        gate, val = conv(c * fc), conv(d_ff + c * fc)
        acts.append((gate / (1.0 + jnp.exp(-gate)) * val).astype(BF16))
    us_ref[0:HALO, :] = us_ref[tm:tm + HALO, :]
    o_ref[...] += jnp.dot(jnp.concatenate(acts, axis=1), wdown_ref[...], preferred_element_type=F32)
    if final_norm:
        o_ref[...] = _rms(o_ref[...], finw_ref[...])


def _out_ffn(x2d, att, hm, w_out, ffn_norm_w, w_up, conv_w, conv_b, w_down, final_w, seq_len, final_norm):
    T, D = x2d.shape
    tm = FFN_TILE
    assert T % tm == 0 and seq_len % tm == 0
    d_ff = w_down.shape[0]
    assert d_ff % FFN_CHUNK == 0
    tps = seq_len // tm
    row = lambda i: (i, 0)
    const2 = lambda i: (0, 0)
    blk3 = lambda i: (0, i, 0)
    once = dict(pipeline_mode=pl.Buffered(1))
    assert tm == PERM_TILE
    permt = jnp.asarray(_dilation_permutation(PERM_TILE).T, BF16)
    return pl.pallas_call(
        functools.partial(_ffn_kernel, tiles_per_seq=seq_len // tm, final_norm=final_norm),
        grid=(T // tm,),
        in_specs=[
            pl.BlockSpec((tm, D), row),
            pl.BlockSpec((ATT_PAIRS, tm // MAX_DIL, MAX_DIL * LANES), blk3),
            pl.BlockSpec((MLSTM_HEADS, None, MLSTM_HEAD_DIM, tm), lambda i: (0, i // tps, 0, i % tps)),
            pl.BlockSpec(permt.shape, const2),
            pl.BlockSpec(w_out.shape, const2, **once),
            pl.BlockSpec((1, D), const2),
            pl.BlockSpec(w_up.shape, const2, **once),
            pl.BlockSpec(conv_w.shape, const2),
            pl.BlockSpec(conv_b.shape, const2),
            pl.BlockSpec(w_down.shape, const2, **once),
            pl.BlockSpec((1, D), const2),
        ],
        out_specs=pl.BlockSpec((tm, D), row),
        out_shape=jax.ShapeDtypeStruct((T, D), F32),
        scratch_shapes=[pltpu.VMEM((HALO + tm, 2 * d_ff), F32)],
        compiler_params=pltpu.CompilerParams(
            dimension_semantics=("arbitrary",), vmem_limit_bytes=VMEM_LIMIT),
        name="out_proj_conv_ffn",
    )(x2d, att, hm, permt, w_out, ffn_norm_w, w_up, conv_w, conv_b, w_down, final_w)


def kernel(x, w_in, mlstm_conv_w, mlstm_conv_b, mlstm_i_bias, mlstm_f_bias, att_out_norm_w, mlstm_out_norm_w, w_out, mixer_norm_w, ffn_norm_w, w_ffn_up, ffn_conv_w, ffn_conv_b, w_ffn_down, final_norm_w):
    batch, seq_len, d_model = x.shape
    depth = w_in.shape[0]
    h = x.reshape(batch * seq_len, d_model)
    for layer in range(depth):
        w = w_in[layer]
        n_nat = 3 * ATT_WIDTH + 2 * MLSTM_WIDTH
        H = MLSTM_HEADS
        gate_rows = jnp.zeros((2 * SUBLANES, d_model), w.dtype)
        gate_rows = gate_rows.at[0:H].set(w[:, n_nat + 2 * MLSTM_WIDTH:n_nat + 2 * MLSTM_WIDTH + H].T)
        gate_rows = gate_rows.at[SUBLANES:SUBLANES + H].set(w[:, n_nat + 2 * MLSTM_WIDTH + H:].T)
        w_t = jnp.concatenate([w[:, n_nat:n_nat + 2 * MLSTM_WIDTH].T, gate_rows], axis=0).astype(BF16)
        gate_b = jnp.zeros((2 * SUBLANES, 1), F32)
        gate_b = gate_b.at[0:H, 0].set(mlstm_i_bias[layer]).at[SUBLANES:SUBLANES + H, 0].set(mlstm_f_bias[layer])
        aq, ak, av, mq, mk, mv, mo, gates = _input_projection(
            h, mixer_norm_w[layer][None, :], w[:, :n_nat].astype(BF16), w_t, mlstm_conv_w[layer],
            mlstm_conv_b[layer][None, :], gate_b, batch, seq_len)
        att = _dilated_attention(aq, ak, av, att_out_norm_w[layer], batch, seq_len)
        hm = _mlstm(mq, mk, mv, mo, gates, mlstm_out_norm_w[layer], batch, seq_len)
        h = _out_ffn(h, att, hm, w_out[layer].astype(BF16), ffn_norm_w[layer][None, :], w_ffn_up[layer].astype(BF16),
                     ffn_conv_w[layer], ffn_conv_b[layer][None, :], w_ffn_down[layer].astype(BF16),
                     final_norm_w[None, :], seq_len, final_norm=(layer == depth - 1))
    return h.reshape(batch, seq_len, d_model)
```

```python
import functools

import numpy as np
import jax
import jax.numpy as jnp
from jax import lax
from jax.experimental import pallas as pl
from jax.experimental.pallas import tpu as pltpu

F32 = jnp.float32
BF16 = jnp.bfloat16

EPS = 1e-6
LANES = 128
ATT_HEADS = 8
ATT_HEAD_DIM = 64
ATT_WIDTH = ATT_HEADS * ATT_HEAD_DIM
ATT_PAIRS = ATT_WIDTH // LANES
ATT_BLOCK = 128
DILATIONS = (16, 4, 1)
MAX_DIL = 16
ATT_TILE = ATT_BLOCK * MAX_DIL
MLSTM_HEADS = 4
MLSTM_HEAD_DIM = 128
MLSTM_WIDTH = MLSTM_HEADS * MLSTM_HEAD_DIM
MLSTM_KERNEL_CHUNK = 128
MLSTM_CONV = 4
FFN_CONV = 3
SUBLANES = 8
HALO = SUBLANES
NEG = -1e30
LOG2E = 1.4426950408889634
VMEM_LIMIT = 56 * 1024 * 1024

PROJ_TILE = 1024
PERM_TILE = 512
FFN_TILE = 512
FFN_CHUNK = 256
MLSTM_TILE = 512
MLSTM_BATCHES = 4


def _rms(x, w):
    return x * lax.rsqrt(jnp.mean(x * x, axis=-1, keepdims=True) + EPS) * w


def _dilation_permutation(tile):
    rows = tile // MAX_DIL
    n = np.arange(tile)
    p = np.zeros((tile, tile), np.float32)
    p[n, MAX_DIL * (n % rows) + n // rows] = 1.0
    return p


def _proj_kernel(x_ref, nw_ref, w_ref, wt_ref, perm_ref, cw_ref, cb_ref, gb_ref,
                 aq_ref, ak_ref, av_ref, mq_ref, mk_ref, mv_ref, mo_ref, g_ref,
                 cs_ref, *, tiles_per_seq):
    tm = x_ref.shape[0]
    i = pl.program_id(0)
    xn = _rms(x_ref[...], nw_ref[...]).astype(BF16)
    xp = jnp.concatenate(
        [jnp.dot(perm_ref[...], xn[PERM_TILE * g:PERM_TILE * (g + 1)], preferred_element_type=F32).astype(BF16)
         for g in range(tm // PERM_TILE)], axis=0)
    base = 3 * ATT_WIDTH

    def seg(lhs, c0, width):
        return lambda: jnp.dot(lhs, w_ref[:, c0:c0 + width], preferred_element_type=F32)

    def put_blocks(ref, y):
        for p in range(ref.shape[0]):
            ref[p] = y[:, LANES * p:LANES * (p + 1)].astype(BF16)

    def put_view(ref, y):
        rows = PERM_TILE // MAX_DIL
        for g in range(tm // PERM_TILE):
            for p in range(ATT_PAIRS):
                for r in range(MAX_DIL):
                    src = PERM_TILE * g + rows * r
                    ref[p, rows * g:rows * (g + 1), LANES * r:LANES * (r + 1)] = (
                        y[src:src + rows, LANES * p:LANES * (p + 1)].astype(BF16))

    @pl.when(i % tiles_per_seq == 0)
    def _():
        cs_ref[0:HALO, :] = jnp.zeros((HALO, cs_ref.shape[1]), F32)

    def put_transposed(ref, y):
        for p in range(ref.shape[0]):
            ref[p] = jnp.transpose(y[:, LANES * p:LANES * (p + 1)]).astype(BF16)

    def conv_silu(y):
        cs_ref[HALO:HALO + tm, :] = y
        y = cb_ref[...]
        for j in range(MLSTM_CONV):
            off = HALO - (MLSTM_CONV - 1) + j
            y = y + cs_ref[off:off + tm, :] * cw_ref[j:j + 1, :]
        cs_ref[0:HALO, :] = cs_ref[tm:tm + HALO, :]
        y = y / (1.0 + jnp.exp(-y))
        put_transposed(mq_ref, y[:, :MLSTM_WIDTH])
        put_blocks(mk_ref, y[:, MLSTM_WIDTH:] * (MLSTM_HEAD_DIM ** -0.5))

    def seg_t(r0, rows):
        return lambda: lax.dot_general(wt_ref[r0:r0 + rows, :], xn, (((1,), (1,)), ((), ())),
                                       preferred_element_type=F32)

    def put_rows(ref):
        def f(y):
            for p in range(ref.shape[0]):
                ref[p] = y[LANES * p:LANES * (p + 1), :].astype(BF16)
        return f

    def gates(z):
        z = z + gb_ref[...]
        g_ref[0] = z[0:SUBLANES]
        zf = z[SUBLANES:2 * SUBLANES]
        g_ref[1] = jnp.minimum(zf, 0.0) - jnp.log1p(jnp.exp(-jnp.abs(zf)))

    stages = [
        (seg(xp, 0, ATT_WIDTH), lambda y: put_view(aq_ref, y * (ATT_HEAD_DIM ** -0.5 * LOG2E))),
        (seg(xp, ATT_WIDTH, ATT_WIDTH), lambda y: put_view(ak_ref, y)),
        (seg(xp, 2 * ATT_WIDTH, ATT_WIDTH), lambda y: put_view(av_ref, y)),
        (seg(xn, base, 2 * MLSTM_WIDTH), conv_silu),
        (seg_t(0, MLSTM_WIDTH), put_rows(mv_ref)),
        (seg_t(MLSTM_WIDTH, MLSTM_WIDTH), put_rows(mo_ref)),
        (seg_t(2 * MLSTM_WIDTH, 2 * SUBLANES), gates),
    ]
    y_next = stages[0][0]()
    for n, (_, epilogue) in enumerate(stages):
        y = y_next
        if n + 1 < len(stages):
            y_next = stages[n + 1][0]()
        epilogue(y)


def _input_projection(x2d, norm_w, w_nat, w_t, conv_w, conv_b, gate_b, batch, seq_len):
    T, D = x2d.shape
    tm = PROJ_TILE
    assert T % tm == 0 and seq_len % tm == 0 and tm % PERM_TILE == 0
    tps = seq_len // tm
    row = lambda i: (i, 0)
    const2 = lambda i: (0, 0)
    blk3 = lambda i: (0, i, 0)
    perm = jnp.asarray(_dilation_permutation(PERM_TILE), BF16)
    H, Dh = MLSTM_HEADS, MLSTM_HEAD_DIM
    heads_shape = jax.ShapeDtypeStruct((H, T, Dh), BF16)
    heads_spec = pl.BlockSpec((H, tm, Dh), blk3)
    heads_t_shape = jax.ShapeDtypeStruct((H, batch, Dh, seq_len), BF16)
    heads_t_spec = pl.BlockSpec((H, None, Dh, tm), lambda i: (0, i // tps, 0, i % tps))
    view_shape = jax.ShapeDtypeStruct((ATT_PAIRS, T // MAX_DIL, MAX_DIL * LANES), BF16)
    view_spec = pl.BlockSpec((ATT_PAIRS, tm // MAX_DIL, MAX_DIL * LANES), blk3)
    gate_shape = jax.ShapeDtypeStruct((batch, 2, SUBLANES, seq_len), F32)
    gate_spec = pl.BlockSpec((None, 2, SUBLANES, tm), lambda i: (i // tps, 0, 0, i % tps))
    return pl.pallas_call(
        functools.partial(_proj_kernel, tiles_per_seq=tps),
        grid=(T // tm,),
        in_specs=[
            pl.BlockSpec((tm, D), row),
            pl.BlockSpec((1, D), const2),
            pl.BlockSpec(w_nat.shape, const2),
            pl.BlockSpec(w_t.shape, const2),
            pl.BlockSpec(perm.shape, const2),
            pl.BlockSpec(conv_w.shape, const2),
            pl.BlockSpec(conv_b.shape, const2),
            pl.BlockSpec(gate_b.shape, const2),
        ],
        out_specs=[view_spec] * 3 + [heads_t_spec, heads_spec, heads_t_spec, heads_t_spec, gate_spec],
        out_shape=[view_shape] * 3 + [heads_t_shape, heads_shape, heads_t_shape, heads_t_shape, gate_shape],
        scratch_shapes=[pltpu.VMEM((HALO + tm, 2 * MLSTM_WIDTH), F32)],
        compiler_params=pltpu.CompilerParams(
            dimension_semantics=("arbitrary",), vmem_limit_bytes=VMEM_LIMIT),
        name="input_projection",
    )(x2d, norm_w, w_nat, w_t, perm, conv_w, conv_b, gate_b)


def _attention_biases():
    n = np.arange(ATT_BLOCK)
    s = np.arange(2 * ATT_BLOCK)
    out = np.zeros((3, 2, 2 * ATT_BLOCK, 2 * ATT_BLOCK), np.float32)
    pq16, pk16 = n, s
    pq4 = 4 * (n % 32) + n // 32
    pk4 = 4 * (s % 64) + s // 64
    perm1 = 16 * (n % 8) + n // 8
    pq1 = perm1
    pk1 = perm1[s % ATT_BLOCK] + ATT_BLOCK * (s // ATT_BLOCK)
    shifts = (ATT_BLOCK, 4 * 32, ATT_BLOCK)
    for g, (pq, pk, shift) in enumerate(((pq16, pk16, shifts[0]), (pq4, pk4, shifts[1]), (pq1, pk1, shifts[2]))):
        for variant, sh in enumerate((shift, 0)):
            dist = pq[:, None] - (pk[None, :] - sh)
            ok = (dist >= 0) & (dist <= ATT_BLOCK)
            bias = np.where(ok, 0.0, NEG).astype(np.float32)
            out[g, variant] = np.concatenate([bias, bias], axis=0)
    return out


def _attn_unit(q, kk, vv, bias, half0):
    zero = jnp.zeros_like(q)
    q2 = jnp.concatenate([jnp.where(half0, q, zero), jnp.where(half0, zero, q)], axis=0)
    s = lax.dot_general(q2, kk, (((1,), (1,)), ((), ())), preferred_element_type=F32) + bias
    m = jnp.max(s, axis=-1, keepdims=True)
    p = jnp.exp2(s - m).astype(BF16)
    vext = jnp.concatenate([vv, jnp.ones_like(vv)], axis=1)
    res = jnp.dot(p, vext, preferred_element_type=F32)
    top, bot = res[:ATT_BLOCK], res[ATT_BLOCK:]
    o = jnp.where(half0, top[:, :LANES], bot[:, :LANES])
    den = jnp.where(half0, top[:, LANES:], bot[:, LANES:])
    mb = jnp.where(half0, m[:ATT_BLOCK], m[ATT_BLOCK:])
    return o, mb, den


def _attention_kernel(q_ref, k_ref, v_ref, bias_ref, nw_ref, o_ref, qf_ref, kf_ref, vf_ref, st_ref):
    jt = pl.program_id(2)
    half0 = lax.broadcasted_iota(jnp.int32, (ATT_BLOCK, LANES), 1) < ATT_HEAD_DIM
    row0 = pl.multiple_of(jt * ATT_BLOCK, ATT_BLOCK)

    def lanes(r):
        return slice(LANES * r, LANES * (r + 1))

    @pl.when(jt == 0)
    def _():
        qf_ref[...] = q_ref[...].astype(F32)
        kf_ref[...] = k_ref[...].astype(F32)
        vf_ref[...] = v_ref[...].astype(F32)

    def store_stats(g, r, rows, stats):
        for st, val in enumerate(stats):
            st_ref[g, st, r, rows, :] = val

    first16 = jt == 0
    k0 = pl.multiple_of(jnp.maximum(row0 - ATT_BLOCK, 0), ATT_BLOCK)
    bias16 = bias_ref[0, jnp.where(first16, 1, 0)]
    for r in range(MAX_DIL):
        q = q_ref[pl.ds(row0, ATT_BLOCK), lanes(r)]
        kk = k_ref[pl.ds(k0, 2 * ATT_BLOCK), lanes(r)]
        vv = v_ref[pl.ds(k0, 2 * ATT_BLOCK), lanes(r)]
        store_stats(0, r, slice(None), _attn_unit(q, kk, vv, bias16, half0))

    def body4(bb, carry):
        gb = jt * 4 + bb
        q0 = pl.multiple_of(gb * 32, 32)
        k0 = pl.multiple_of(jnp.maximum(gb * 32 - 32, 0), 32)
        bias = bias_ref[1, jnp.where(gb == 0, 1, 0)]
        o0 = pl.multiple_of(bb * 32, 32)
        for r0 in range(4):
            q = jnp.concatenate([q_ref[pl.ds(q0, 32), lanes(4 * r1 + r0)] for r1 in range(4)], axis=0)
            kk = jnp.concatenate([k_ref[pl.ds(k0, 64), lanes(4 * r1 + r0)] for r1 in range(4)], axis=0)
            vv = jnp.concatenate([v_ref[pl.ds(k0, 64), lanes(4 * r1 + r0)] for r1 in range(4)], axis=0)
            stats = _attn_unit(q, kk, vv, bias, half0)
            for r1 in range(4):
                store_stats(1, 4 * r1 + r0, pl.ds(o0, 32), [s[32 * r1:32 * (r1 + 1)] for s in stats])
        return carry

    lax.fori_loop(0, 4, body4, 0, unroll=True)

    def body1(bl, carry):
        b = jt * MAX_DIL + bl
        q0 = pl.multiple_of(b * 8, 8)
        k0 = pl.multiple_of(jnp.maximum(b * 8 - 8, 0), 8)
        bias = bias_ref[2, jnp.where(b == 0, 1, 0)]
        o0 = pl.multiple_of(bl * 8, 8)

        def gather(ref, start):
            return jnp.concatenate([ref[pl.ds(start, 8), lanes(r)] for r in range(MAX_DIL)], axis=0)

        q = gather(qf_ref, q0).astype(BF16)
        kk = jnp.concatenate([gather(kf_ref, k0), gather(kf_ref, k0 + 8)], axis=0).astype(BF16)
        vv = jnp.concatenate([gather(vf_ref, k0), gather(vf_ref, k0 + 8)], axis=0).astype(BF16)
        stats = _attn_unit(q, kk, vv, bias, half0)
        for r in range(MAX_DIL):
            store_stats(2, r, pl.ds(o0, 8), [s[8 * r:8 * (r + 1)] for s in stats])
        return carry

    lax.fori_loop(0, MAX_DIL, body1, 0, unroll=True)

    nw = nw_ref[...]
    for r in range(MAX_DIL):
        ms = [st_ref[g, 1, r] for g in range(3)]
        mx = jnp.maximum(jnp.maximum(ms[0], ms[1]), ms[2])
        fs = [jnp.exp2(m - mx) for m in ms]
        num = fs[0] * st_ref[0, 0, r] + fs[1] * st_ref[1, 0, r] + fs[2] * st_ref[2, 0, r]
        den = fs[0] * st_ref[0, 2, r] + fs[1] * st_ref[1, 2, r] + fs[2] * st_ref[2, 2, r]
        att = num / den
        a2 = att * att
        s0 = jnp.sum(jnp.where(half0, a2, 0.0), axis=-1, keepdims=True)
        s1 = jnp.sum(jnp.where(half0, 0.0, a2), axis=-1, keepdims=True)
        msq = jnp.where(half0, s0, s1) * (1.0 / ATT_HEAD_DIM)
        o_ref[pl.ds(row0, ATT_BLOCK), lanes(r)] = (att * lax.rsqrt(msq + EPS) * nw).astype(BF16)


def _dilated_attention(aq, ak, av, norm_w, batch, seq_len):
    assert seq_len % ATT_TILE == 0 and seq_len >= 2 * ATT_TILE
    rows = seq_len // MAX_DIL
    nt = seq_len // ATT_TILE
    view = lambda t: t.reshape(ATT_PAIRS, batch, rows, MAX_DIL * LANES)
    bias = jnp.asarray(_attention_biases())
    seq_spec = pl.BlockSpec((None, None, rows, MAX_DIL * LANES), lambda b, p, j: (p, b, 0, 0))
    out = pl.pallas_call(
        _attention_kernel,
        grid=(batch, ATT_PAIRS, nt),
        in_specs=[seq_spec, seq_spec, seq_spec,
                  pl.BlockSpec(bias.shape, lambda b, p, j: (0, 0, 0, 0)),
                  pl.BlockSpec((None, 1, LANES), lambda b, p, j: (p, 0, 0))],
        out_specs=seq_spec,
        out_shape=jax.ShapeDtypeStruct((ATT_PAIRS, batch, rows, MAX_DIL * LANES), BF16),
        scratch_shapes=[pltpu.VMEM((rows, MAX_DIL * LANES), F32)] * 3
        + [pltpu.VMEM((3, 3, MAX_DIL, ATT_BLOCK, LANES), F32)],
        compiler_params=pltpu.CompilerParams(
            dimension_semantics=("arbitrary", "arbitrary", "arbitrary"), vmem_limit_bytes=VMEM_LIMIT),
        name="dilated_attention",
    )(view(aq), view(ak), view(av), bias, norm_w.reshape(ATT_PAIRS, 1, LANES))
    return out.reshape(ATT_PAIRS, batch * rows, MAX_DIL * LANES)


def _scan_lanes(x, op, fill):
    n = x.shape[1]
    lane = lax.broadcasted_iota(jnp.int32, x.shape, 1)
    sh = 1
    while sh < n:
        x = op(x, jnp.where(lane >= sh, pltpu.roll(x, sh, axis=1), fill))
        sh *= 2
    return x


def _two_term_rows(row):
    r = lax.broadcasted_iota(jnp.int32, (2 * SUBLANES, row.shape[1]), 0)
    full = jnp.broadcast_to(row, r.shape)
    hi = full.astype(BF16).astype(F32)
    return jnp.where(r == 0, hi, jnp.where(r == 1, full - hi, 0.0)).astype(BF16)


def _mlstm_kernel(q_ref, k_ref, v_ref, o_ref, g_ref, nw_ref, h_ref, c_ref, n_ref, m_ref, sc_ref, et_ref):
    Lc, D, H = MLSTM_KERNEL_CHUNK, MLSTM_HEAD_DIM, MLSTM_HEADS
    nb, ts = q_ref.shape[1], q_ref.shape[3]

    @pl.when(pl.program_id(1) == 0)
    def _():
        c_ref[...] = jnp.zeros(c_ref.shape, F32)
        n_ref[...] = jnp.zeros(n_ref.shape, F32)
        m_ref[...] = jnp.zeros(m_ref.shape, F32)

    si = lax.broadcasted_iota(jnp.int32, (Lc, Lc), 0)
    ti = lax.broadcasted_iota(jnp.int32, (Lc, Lc), 1)
    causal = si <= ti
    nc = ts // Lc

    tiles = [(c, b) for c in range(nc) for b in range(nb)]
    log_i = jnp.concatenate([g_ref[b, 0, :, Lc * c:Lc * (c + 1)] for c, b in tiles], axis=0)
    log_f = jnp.concatenate([g_ref[b, 1, :, Lc * c:Lc * (c + 1)] for c, b in tiles], axis=0)
    bcum = _scan_lanes(log_f, jnp.add, 0.0)
    e = log_i - bcum
    sc_ref[0] = bcum
    sc_ref[1] = e
    sc_ref[2] = _scan_lanes(e, jnp.maximum, NEG)
    rows_c = nb * SUBLANES
    for c in range(nc):
        e_c = e[rows_c * c:rows_c * (c + 1)]
        et_ref[c] = jnp.transpose(jnp.concatenate([e_c, jnp.zeros((Lc - rows_c, Lc), F32)], axis=0))

    def chunk(c, carry):
        t0 = pl.multiple_of(c * Lc, Lc)
        e_cols = et_ref[c]
        heads = [(b, h) for b in range(nb) for h in range(H)]
        rows = {}
        for b in range(nb):
            r0 = pl.multiple_of((c * nb + b) * SUBLANES, SUBLANES)
            bcum = sc_ref[0, pl.ds(r0, SUBLANES), :]
            e = sc_ref[1, pl.ds(r0, SUBLANES), :]
            m_prev = m_ref[b]
            g = bcum[:, Lc - 1:Lc]
            m_t = bcum + jnp.maximum(m_prev, sc_ref[2, pl.ds(r0, SUBLANES), :])
            c_minus_m = bcum - m_t
            a = g + e
            m_new = jnp.maximum(g + m_prev, jnp.max(a, axis=1, keepdims=True))
            rows[b] = dict(c_minus_m=c_minus_m, inter=jnp.exp(c_minus_m + m_prev), floor=jnp.exp(-m_t),
                           decay=jnp.exp(g + m_prev - m_new), w=jnp.exp(a - m_new))
            m_ref[b] = jnp.broadcast_to(m_new, m_prev.shape)

        qt = {bh: q_ref[bh[1], bh[0], :, pl.ds(t0, Lc)] for bh in heads}
        k = {bh: k_ref[bh[1], bh[0], pl.ds(t0, Lc), :] for bh in heads}
        vt = {bh: v_ref[bh[1], bh[0], :, pl.ds(t0, Lc)] for bh in heads}
        st = {bh: jnp.dot(k[bh], qt[bh], preferred_element_type=F32) for bh in heads}
        cq = {(b, h): jnp.dot(c_ref[b * H + h].astype(BF16), qt[b, h], preferred_element_type=F32) for b, h in heads}
        nq = {(b, h): jnp.dot(_two_term_rows(n_ref[b * H + h, 0:1, :]), qt[b, h], preferred_element_type=F32)
              for b, h in heads}
        scores = {}
        for b, h in heads:
            col = SUBLANES * b + h
            dprime = jnp.where(causal, e_cols[:, col:col + 1] + rows[b]["c_minus_m"][h:h + 1, :], NEG)
            scores[b, h] = st[b, h] * jnp.exp(dprime)
        num = {(b, h): jnp.dot(vt[b, h], scores[b, h].astype(BF16), preferred_element_type=F32)
               + rows[b]["inter"][h:h + 1, :] * cq[b, h] for b, h in heads}
        for b, h in heads:
            inter = rows[b]["inter"][h:h + 1, :]
            den = jnp.sum(scores[b, h], axis=0, keepdims=True) + inter * (nq[b, h][0:1, :] + nq[b, h][1:2, :])
            hh = num[b, h] * (1.0 / jnp.maximum(jnp.abs(den), rows[b]["floor"][h:h + 1, :]))
            hh = hh * lax.rsqrt(jnp.mean(hh * hh, axis=0, keepdims=True) + EPS) * nw_ref[h]
            gate = jax.nn.sigmoid(o_ref[h, b, :, pl.ds(t0, Lc)].astype(F32))
            h_ref[h, b, :, pl.ds(t0, Lc)] = (gate * hh).astype(BF16)
        for b, h in heads:
            s = b * H + h
            decay = rows[b]["decay"][h:h + 1, :]
            w_row = rows[b]["w"][h:h + 1, :]
            dc = jnp.dot((vt[b, h].astype(F32) * w_row).astype(BF16), k[b, h], preferred_element_type=F32)
            c_ref[s] = decay * c_ref[s] + dc
            dn = jnp.dot(_two_term_rows(w_row), k[b, h], preferred_element_type=F32)
            n_ref[s, 0:1, :] = decay * n_ref[s, 0:1, :] + dn[0:1, :] + dn[1:2, :]
        return carry

    lax.fori_loop(0, ts // Lc, chunk, 0)


def _mlstm(mqt, mk, mvt, mot, gates, norm_w, batch, seq_len):
    H, D = MLSTM_HEADS, MLSTM_HEAD_DIM
    ts = MLSTM_TILE
    nb = MLSTM_BATCHES if batch % MLSTM_BATCHES == 0 else 1
    assert seq_len % ts == 0 and ts % MLSTM_KERNEL_CHUNK == 0 and D == MLSTM_KERNEL_CHUNK
    nt = seq_len // ts
    tspec = pl.BlockSpec((H, nb, D, ts), lambda b, j: (0, b, 0, j))
    nw_cols = jnp.broadcast_to(norm_w.reshape(H, D, 1), (H, D, LANES))
    return pl.pallas_call(
        _mlstm_kernel,
        grid=(batch // nb, nt),
        in_specs=[tspec,
                  pl.BlockSpec((H, nb, ts, D), lambda b, j: (0, b, j, 0)),
                  tspec, tspec,
                  pl.BlockSpec((nb, 2, SUBLANES, ts), lambda b, j: (b, 0, 0, j)),
                  pl.BlockSpec((H, D, LANES), lambda b, j: (0, 0, 0))],
        out_specs=tspec,
        out_shape=jax.ShapeDtypeStruct((H, batch, D, seq_len), BF16),
        scratch_shapes=[pltpu.VMEM((nb * H, D, D), F32), pltpu.VMEM((nb * H, HALO, D), F32),
                        pltpu.VMEM((nb, SUBLANES, LANES), F32),
                        pltpu.VMEM((3, (ts // D) * nb * SUBLANES, D), F32), pltpu.VMEM((ts // D, D, D), F32)],
        compiler_params=pltpu.CompilerParams(
            dimension_semantics=("arbitrary", "arbitrary"), vmem_limit_bytes=VMEM_LIMIT),
        name="mlstm",
    )(mqt, mk.reshape(H, batch, seq_len, D), mvt, mot, gates, nw_cols)


def _ffn_kernel(x_ref, att_ref, hm_ref, permt_ref, wout_ref, fnw_ref, wup_ref, cw_ref, cb_ref, wdown_ref, finw_ref,
                o_ref, us_ref, *, tiles_per_seq, final_norm):
    tm = x_ref.shape[0]
    fc = FFN_CHUNK
    d_ff = wdown_ref.shape[0]
    nf = d_ff // fc
    i = pl.program_id(0)

    att = jnp.concatenate(
        [jnp.concatenate([att_ref[p, :, LANES * r:LANES * (r + 1)] for r in range(MAX_DIL)], axis=0)
         for p in range(ATT_PAIRS)], axis=1)
    att = jnp.dot(permt_ref[...], att, preferred_element_type=F32).astype(BF16)
    hm = [jnp.transpose(hm_ref[h].astype(F32)).astype(BF16) for h in range(MLSTM_HEADS)]
    y = jnp.concatenate([att] + hm, axis=1)
    h1 = x_ref[...] + jnp.dot(y, wout_ref[...], preferred_element_type=F32)
    o_ref[...] = h1
    xn = _rms(h1, fnw_ref[...]).astype(BF16)

    @pl.when(i % tiles_per_seq == 0)
    def _():
        us_ref[0:HALO, :] = jnp.zeros((HALO, us_ref.shape[1]), F32)

    us_ref[HALO:HALO + tm, :] = jnp.dot(xn, wup_ref[...], preferred_element_type=F32)

    def conv(c0):
        u = cb_ref[:, c0:c0 + fc]
        for j in range(FFN_CONV):
            off = HALO - (FFN_CONV - 1) + j
            u = u + us_ref[off:off + tm, c0:c0 + fc] * cw_ref[j:j + 1, c0:c0 + fc]
        return u

    acts = []
    for c in range(nf):
        gate, val = conv(c * fc), conv(d_ff + c * fc)
        acts.append((gate / (1.0 + jnp.exp(-gate)) * val).astype(BF16))
    us_ref[0:HALO, :] = us_ref[tm:tm + HALO, :]
    o_ref[...] += jnp.dot(jnp.concatenate(acts, axis=1), wdown_ref[...], preferred_element_type=F32)
    if final_norm:
        o_ref[...] = _rms(o_ref[...], finw_ref[...])


def _out_ffn(x2d, att, hm, w_out, ffn_norm_w, w_up, conv_w, conv_b, w_down, final_w, seq_len, final_norm):
    T, D = x2d.shape
    tm = FFN_TILE
    assert T % tm == 0 and seq_len % tm == 0
    d_ff = w_down.shape[0]
    assert d_ff % FFN_CHUNK == 0
    tps = seq_len // tm
    row = lambda i: (i, 0)
    const2 = lambda i: (0, 0)
    blk3 = lambda i: (0, i, 0)
    once = dict(pipeline_mode=pl.Buffered(1))
    assert tm == PERM_TILE
    permt = jnp.asarray(_dilation_permutation(PERM_TILE).T, BF16)
    return pl.pallas_call(
        functools.partial(_ffn_kernel, tiles_per_seq=seq_len // tm, final_norm=final_norm),
        grid=(T // tm,),
        in_specs=[
            pl.BlockSpec((tm, D), row),
            pl.BlockSpec((ATT_PAIRS, tm // MAX_DIL, MAX_DIL * LANES), blk3),
            pl.BlockSpec((MLSTM_HEADS, None, MLSTM_HEAD_DIM, tm), lambda i: (0, i // tps, 0, i % tps)),
            pl.BlockSpec(permt.shape, const2),
            pl.BlockSpec(w_out.shape, const2, **once),
            pl.BlockSpec((1, D), const2),
            pl.BlockSpec(w_up.shape, const2, **once),
            pl.BlockSpec(conv_w.shape, const2),
            pl.BlockSpec(conv_b.shape, const2),
            pl.BlockSpec(w_down.shape, const2, **once),
            pl.BlockSpec((1, D), const2),
        ],
        out_specs=pl.BlockSpec((tm, D), row),
        out_shape=jax.ShapeDtypeStruct((T, D), F32),
        scratch_shapes=[pltpu.VMEM((HALO + tm, 2 * d_ff), F32)],
        compiler_params=pltpu.CompilerParams(
            dimension_semantics=("arbitrary",), vmem_limit_bytes=VMEM_LIMIT),
        name="out_proj_conv_ffn",
    )(x2d, att, hm, permt, w_out, ffn_norm_w, w_up, conv_w, conv_b, w_down, final_w)


def kernel(x, w_in, mlstm_conv_w, mlstm_conv_b, mlstm_i_bias, mlstm_f_bias, att_out_norm_w, mlstm_out_norm_w, w_out, mixer_norm_w, ffn_norm_w, w_ffn_up, ffn_conv_w, ffn_conv_b, w_ffn_down, final_norm_w):
    batch, seq_len, d_model = x.shape
    depth = w_in.shape[0]
    h = x.reshape(batch * seq_len, d_model)
    for layer in range(depth):
        w = w_in[layer]
        n_nat = 3 * ATT_WIDTH + 2 * MLSTM_WIDTH
        H = MLSTM_HEADS
        gate_rows = jnp.zeros((2 * SUBLANES, d_model), w.dtype)
        gate_rows = gate_rows.at[0:H].set(w[:, n_nat + 2 * MLSTM_WIDTH:n_nat + 2 * MLSTM_WIDTH + H].T)
        gate_rows = gate_rows.at[SUBLANES:SUBLANES + H].set(w[:, n_nat + 2 * MLSTM_WIDTH + H:].T)
        w_t = jnp.concatenate([w[:, n_nat:n_nat + 2 * MLSTM_WIDTH].T, gate_rows], axis=0).astype(BF16)
        gate_b = jnp.zeros((2 * SUBLANES, 1), F32)
        gate_b = gate_b.at[0:H, 0].set(mlstm_i_bias[layer]).at[SUBLANES:SUBLANES + H, 0].set(mlstm_f_bias[layer])
        aq, ak, av, mq, mk, mv, mo, gates = _input_projection(
            h, mixer_norm_w[layer][None, :], w[:, :n_nat].astype(BF16), w_t, mlstm_conv_w[layer],
            mlstm_conv_b[layer][None, :], gate_b, batch, seq_len)
        att = _dilated_attention(aq, ak, av, att_out_norm_w[layer], batch, seq_len)
        hm = _mlstm(mq, mk, mv, mo, gates, mlstm_out_norm_w[layer], batch, seq_len)
        h = _out_ffn(h, att, hm, w_out[layer].astype(BF16), ffn_norm_w[layer][None, :], w_ffn_up[layer].astype(BF16),
                     ffn_conv_w[layer], ffn_conv_b[layer][None, :], w_ffn_down[layer].astype(BF16),
                     final_norm_w[None, :], seq_len, final_norm=(layer == depth - 1))
    return h.reshape(batch, seq_len, d_model)
```

```python
import functools

import numpy as np
import jax
import jax.numpy as jnp
from jax import lax
from jax.experimental import pallas as pl
from jax.experimental.pallas import tpu as pltpu

F32 = jnp.float32
BF16 = jnp.bfloat16

EPS = 1e-6
LANES = 128
ATT_HEADS = 8
ATT_HEAD_DIM = 64
ATT_WIDTH = ATT_HEADS * ATT_HEAD_DIM
ATT_PAIRS = ATT_WIDTH // LANES
ATT_BLOCK = 128
DILATIONS = (16, 4, 1)
MAX_DIL = 16
ATT_TILE = ATT_BLOCK * MAX_DIL
MLSTM_HEADS = 4
MLSTM_HEAD_DIM = 128
MLSTM_WIDTH = MLSTM_HEADS * MLSTM_HEAD_DIM
MLSTM_KERNEL_CHUNK = 128
MLSTM_CONV = 4
FFN_CONV = 3
SUBLANES = 8
HALO = SUBLANES
NEG = -1e30
LOG2E = 1.4426950408889634
VMEM_LIMIT = 56 * 1024 * 1024

PROJ_TILE = 1024
PERM_TILE = 512
FFN_TILE = 512
FFN_CHUNK = 256
MLSTM_TILE = 512
MLSTM_BATCHES = 8


def _rms(x, w):
    return x * lax.rsqrt(jnp.mean(x * x, axis=-1, keepdims=True) + EPS) * w


def _dilation_permutation(tile):
    rows = tile // MAX_DIL
    n = np.arange(tile)
    p = np.zeros((tile, tile), np.float32)
    p[n, MAX_DIL * (n % rows) + n // rows] = 1.0
    return p


def _proj_kernel(x_ref, nw_ref, w_ref, wt_ref, perm_ref, cw_ref, cb_ref, gb_ref,
                 aq_ref, ak_ref, av_ref, mq_ref, mk_ref, mv_ref, mo_ref, g_ref,
                 cs_ref, *, tiles_per_seq):
    tm = x_ref.shape[0]
    i = pl.program_id(0)
    xn = _rms(x_ref[...], nw_ref[...]).astype(BF16)
    xp = jnp.concatenate(
        [jnp.dot(perm_ref[...], xn[PERM_TILE * g:PERM_TILE * (g + 1)], preferred_element_type=F32).astype(BF16)
         for g in range(tm // PERM_TILE)], axis=0)
    base = 3 * ATT_WIDTH

    def seg(lhs, c0, width):
        return lambda: jnp.dot(lhs, w_ref[:, c0:c0 + width], preferred_element_type=F32)

    def put_blocks(ref, y):
        for p in range(ref.shape[0]):
            ref[p] = y[:, LANES * p:LANES * (p + 1)].astype(BF16)

    def put_view(ref, y):
        rows = PERM_TILE // MAX_DIL
        for g in range(tm // PERM_TILE):
            for p in range(ATT_PAIRS):
                for r in range(MAX_DIL):
                    src = PERM_TILE * g + rows * r
                    ref[p, rows * g:rows * (g + 1), LANES * r:LANES * (r + 1)] = (
                        y[src:src + rows, LANES * p:LANES * (p + 1)].astype(BF16))

    @pl.when(i % tiles_per_seq == 0)
    def _():
        cs_ref[:, HALO:2 * HALO, :] = jnp.zeros((MLSTM_CONV, HALO, cs_ref.shape[2]), F32)

    def put_transposed(ref, y):
        for p in range(ref.shape[0]):
            ref[p] = jnp.transpose(y[:, LANES * p:LANES * (p + 1)]).astype(BF16)

    def conv_silu(y):
        for j in range(MLSTM_CONV):
            sh = MLSTM_CONV - 1 - j
            cs_ref[j, HALO + sh:HALO + sh + tm, :] = y
        y = cb_ref[...]
        for j in range(MLSTM_CONV):
            y = y + cs_ref[j, HALO:HALO + tm, :] * cw_ref[j:j + 1, :]
        for j in range(MLSTM_CONV):
            cs_ref[j, HALO:2 * HALO, :] = cs_ref[j, HALO + tm:2 * HALO + tm, :]
        y = y / (1.0 + jnp.exp(-y))
        put_transposed(mq_ref, y[:, :MLSTM_WIDTH])
        put_blocks(mk_ref, y[:, MLSTM_WIDTH:] * (MLSTM_HEAD_DIM ** -0.5))

    def seg_t(r0, rows):
        return lambda: lax.dot_general(wt_ref[r0:r0 + rows, :], xn, (((1,), (1,)), ((), ())),
                                       preferred_element_type=F32)

    def put_rows(ref):
        def f(y):
            for p in range(ref.shape[0]):
                ref[p] = y[LANES * p:LANES * (p + 1), :].astype(BF16)
        return f

    def gates(z):
        z = jnp.transpose(z + gb_ref[...])
        g_ref[0] = z[0:SUBLANES]
        zf = z[SUBLANES:2 * SUBLANES]
        g_ref[1] = jnp.minimum(zf, 0.0) - jnp.log1p(jnp.exp(-jnp.abs(zf)))

    stages = [
        (seg(xp, 0, ATT_WIDTH), lambda y: put_view(aq_ref, y * (ATT_HEAD_DIM ** -0.5 * LOG2E))),
        (seg(xp, ATT_WIDTH, ATT_WIDTH), lambda y: put_view(ak_ref, y)),
        (seg(xp, 2 * ATT_WIDTH, ATT_WIDTH), lambda y: put_view(av_ref, y)),
        (seg(xn, base, 2 * MLSTM_WIDTH), conv_silu),
        (seg_t(0, MLSTM_WIDTH), put_rows(mv_ref)),
        (seg_t(MLSTM_WIDTH, MLSTM_WIDTH), put_rows(mo_ref)),
        (seg(xn, base + 2 * MLSTM_WIDTH, LANES), gates),
    ]
    y_next = stages[0][0]()
    for n, (_, epilogue) in enumerate(stages):
        y = y_next
        if n + 1 < len(stages):
            y_next = stages[n + 1][0]()
        epilogue(y)


def _input_projection(x2d, norm_w, w_nat, w_t, conv_w, conv_b, gate_b, batch, seq_len):
    T, D = x2d.shape
    tm = PROJ_TILE
    assert T % tm == 0 and seq_len % tm == 0 and tm % PERM_TILE == 0
    tps = seq_len // tm
    row = lambda i: (i, 0)
    const2 = lambda i: (0, 0)
    blk3 = lambda i: (0, i, 0)
    perm = jnp.asarray(_dilation_permutation(PERM_TILE), BF16)
    H, Dh = MLSTM_HEADS, MLSTM_HEAD_DIM
    heads_shape = jax.ShapeDtypeStruct((H, T, Dh), BF16)
    heads_spec = pl.BlockSpec((H, tm, Dh), blk3)
    heads_t_shape = jax.ShapeDtypeStruct((H, batch, Dh, seq_len), BF16)
    heads_t_spec = pl.BlockSpec((H, None, Dh, tm), lambda i: (0, i // tps, 0, i % tps))
    view_shape = jax.ShapeDtypeStruct((ATT_PAIRS, T // MAX_DIL, MAX_DIL * LANES), BF16)
    view_spec = pl.BlockSpec((ATT_PAIRS, tm // MAX_DIL, MAX_DIL * LANES), blk3)
    gate_shape = jax.ShapeDtypeStruct((batch, 2, SUBLANES, seq_len), F32)
    gate_spec = pl.BlockSpec((None, 2, SUBLANES, tm), lambda i: (i // tps, 0, 0, i % tps))
    return pl.pallas_call(
        functools.partial(_proj_kernel, tiles_per_seq=tps),
        grid=(T // tm,),
        in_specs=[
            pl.BlockSpec((tm, D), row),
            pl.BlockSpec((1, D), const2),
            pl.BlockSpec(w_nat.shape, const2),
            pl.BlockSpec(w_t.shape, const2),
            pl.BlockSpec(perm.shape, const2),
            pl.BlockSpec(conv_w.shape, const2),
            pl.BlockSpec(conv_b.shape, const2),
            pl.BlockSpec(gate_b.shape, const2),
        ],
        out_specs=[view_spec] * 3 + [heads_t_spec, heads_spec, heads_t_spec, heads_t_spec, gate_spec],
        out_shape=[view_shape] * 3 + [heads_t_shape, heads_shape, heads_t_shape, heads_t_shape, gate_shape],
        scratch_shapes=[pltpu.VMEM((MLSTM_CONV, 3 * HALO + tm, 2 * MLSTM_WIDTH), F32)],
        compiler_params=pltpu.CompilerParams(
            dimension_semantics=("arbitrary",), vmem_limit_bytes=VMEM_LIMIT),
        name="input_projection",
    )(x2d, norm_w, w_nat, w_t, perm, conv_w, conv_b, gate_b)


def _attention_biases():
    n = np.arange(ATT_BLOCK)
    s = np.arange(2 * ATT_BLOCK)
    out = np.zeros((3, 2, 2 * ATT_BLOCK, 2 * ATT_BLOCK), np.float32)
    pq16, pk16 = n, s
    pq4 = 4 * (n % 32) + n // 32
    pk4 = 4 * (s % 64) + s // 64
    perm1 = 16 * (n % 8) + n // 8
    pq1 = perm1
    pk1 = perm1[s % ATT_BLOCK] + ATT_BLOCK * (s // ATT_BLOCK)
    shifts = (ATT_BLOCK, 4 * 32, ATT_BLOCK)
    for g, (pq, pk, shift) in enumerate(((pq16, pk16, shifts[0]), (pq4, pk4, shifts[1]), (pq1, pk1, shifts[2]))):
        for variant, sh in enumerate((shift, 0)):
            dist = pq[:, None] - (pk[None, :] - sh)
            ok = (dist >= 0) & (dist <= ATT_BLOCK)
            bias = np.where(ok, 0.0, NEG).astype(np.float32)
            out[g, variant] = np.concatenate([bias, bias], axis=0)
    return out


def _attn_unit(q, kk, vv, bias, half0):
    zero = jnp.zeros_like(q)
    q2 = jnp.concatenate([jnp.where(half0, q, zero), jnp.where(half0, zero, q)], axis=0)
    s = lax.dot_general(q2, kk, (((1,), (1,)), ((), ())), preferred_element_type=F32) + bias
    m = jnp.max(s, axis=-1, keepdims=True)
    p = jnp.exp2(s - m).astype(BF16)
    vext = jnp.concatenate([vv, jnp.ones_like(vv)], axis=1)
    res = jnp.dot(p, vext, preferred_element_type=F32)
    top, bot = res[:ATT_BLOCK], res[ATT_BLOCK:]
    o = jnp.where(half0, top[:, :LANES], bot[:, :LANES])
    den = jnp.where(half0, top[:, LANES:], bot[:, LANES:])
    mb = jnp.where(half0, m[:ATT_BLOCK], m[ATT_BLOCK:])
    return o, mb, den


def _attention_kernel(q_ref, k_ref, v_ref, bias_ref, nw_ref, o_ref, qf_ref, kf_ref, vf_ref, st_ref):
    jt = pl.program_id(2)
    half0 = lax.broadcasted_iota(jnp.int32, (ATT_BLOCK, LANES), 1) < ATT_HEAD_DIM
    row0 = pl.multiple_of(jt * ATT_BLOCK, ATT_BLOCK)

    def lanes(r):
        return slice(LANES * r, LANES * (r + 1))

    @pl.when(jt == 0)
    def _():
        qf_ref[...] = q_ref[...].astype(F32)
        kf_ref[...] = k_ref[...].astype(F32)
        vf_ref[...] = v_ref[...].astype(F32)

    def store_stats(g, r, rows, stats):
        for st, val in enumerate(stats):
            st_ref[g, st, r, rows, :] = val

    first16 = jt == 0
    k0 = pl.multiple_of(jnp.maximum(row0 - ATT_BLOCK, 0), ATT_BLOCK)
    bias16 = bias_ref[0, jnp.where(first16, 1, 0)]
    for r in range(MAX_DIL):
        q = q_ref[pl.ds(row0, ATT_BLOCK), lanes(r)]
        kk = k_ref[pl.ds(k0, 2 * ATT_BLOCK), lanes(r)]
        vv = v_ref[pl.ds(k0, 2 * ATT_BLOCK), lanes(r)]
        store_stats(0, r, slice(None), _attn_unit(q, kk, vv, bias16, half0))

    def body4(bb, carry):
        gb = jt * 4 + bb
        q0 = pl.multiple_of(gb * 32, 32)
        k0 = pl.multiple_of(jnp.maximum(gb * 32 - 32, 0), 32)
        bias = bias_ref[1, jnp.where(gb == 0, 1, 0)]
        o0 = pl.multiple_of(bb * 32, 32)
        for r0 in range(4):
            q = jnp.concatenate([q_ref[pl.ds(q0, 32), lanes(4 * r1 + r0)] for r1 in range(4)], axis=0)
            kk = jnp.concatenate([k_ref[pl.ds(k0, 64), lanes(4 * r1 + r0)] for r1 in range(4)], axis=0)
            vv = jnp.concatenate([v_ref[pl.ds(k0, 64), lanes(4 * r1 + r0)] for r1 in range(4)], axis=0)
            stats = _attn_unit(q, kk, vv, bias, half0)
            for r1 in range(4):
                store_stats(1, 4 * r1 + r0, pl.ds(o0, 32), [s[32 * r1:32 * (r1 + 1)] for s in stats])
        return carry

    lax.fori_loop(0, 4, body4, 0, unroll=True)

    def body1(bl, carry):
        b = jt * MAX_DIL + bl
        q0 = pl.multiple_of(b * 8, 8)
        k0 = pl.multiple_of(jnp.maximum(b * 8 - 8, 0), 8)
        bias = bias_ref[2, jnp.where(b == 0, 1, 0)]
        o0 = pl.multiple_of(bl * 8, 8)

        def gather(ref, start):
            return jnp.concatenate([ref[pl.ds(start, 8), lanes(r)] for r in range(MAX_DIL)], axis=0)

        q = gather(qf_ref, q0).astype(BF16)
        kk = jnp.concatenate([gather(kf_ref, k0), gather(kf_ref, k0 + 8)], axis=0).astype(BF16)
        vv = jnp.concatenate([gather(vf_ref, k0), gather(vf_ref, k0 + 8)], axis=0).astype(BF16)
        stats = _attn_unit(q, kk, vv, bias, half0)
        for r in range(MAX_DIL):
            store_stats(2, r, pl.ds(o0, 8), [s[8 * r:8 * (r + 1)] for s in stats])
        return carry

    lax.fori_loop(0, MAX_DIL, body1, 0, unroll=True)

    nw = nw_ref[...]
    for r in range(MAX_DIL):
        ms = [st_ref[g, 1, r] for g in range(3)]
        mx = jnp.maximum(jnp.maximum(ms[0], ms[1]), ms[2])
        fs = [jnp.exp2(m - mx) for m in ms]
        num = fs[0] * st_ref[0, 0, r] + fs[1] * st_ref[1, 0, r] + fs[2] * st_ref[2, 0, r]
        den = fs[0] * st_ref[0, 2, r] + fs[1] * st_ref[1, 2, r] + fs[2] * st_ref[2, 2, r]
        att = num / den
        a2 = att * att
        s0 = jnp.sum(jnp.where(half0, a2, 0.0), axis=-1, keepdims=True)
        s1 = jnp.sum(jnp.where(half0, 0.0, a2), axis=-1, keepdims=True)
        msq = jnp.where(half0, s0, s1) * (1.0 / ATT_HEAD_DIM)
        o_ref[pl.ds(row0, ATT_BLOCK), lanes(r)] = (att * lax.rsqrt(msq + EPS) * nw).astype(BF16)


def _dilated_attention(aq, ak, av, norm_w, batch, seq_len):
    assert seq_len % ATT_TILE == 0 and seq_len >= 2 * ATT_TILE
    rows = seq_len // MAX_DIL
    nt = seq_len // ATT_TILE
    view = lambda t: t.reshape(ATT_PAIRS, batch, rows, MAX_DIL * LANES)
    bias = jnp.asarray(_attention_biases())
    seq_spec = pl.BlockSpec((None, None, rows, MAX_DIL * LANES), lambda b, p, j: (p, b, 0, 0))
    out = pl.pallas_call(
        _attention_kernel,
        grid=(batch, ATT_PAIRS, nt),
        in_specs=[seq_spec, seq_spec, seq_spec,
                  pl.BlockSpec(bias.shape, lambda b, p, j: (0, 0, 0, 0)),
                  pl.BlockSpec((None, 1, LANES), lambda b, p, j: (p, 0, 0))],
        out_specs=seq_spec,
        out_shape=jax.ShapeDtypeStruct((ATT_PAIRS, batch, rows, MAX_DIL * LANES), BF16),
        scratch_shapes=[pltpu.VMEM((rows, MAX_DIL * LANES), F32)] * 3
        + [pltpu.VMEM((3, 3, MAX_DIL, ATT_BLOCK, LANES), F32)],
        compiler_params=pltpu.CompilerParams(
            dimension_semantics=("arbitrary", "arbitrary", "arbitrary"), vmem_limit_bytes=VMEM_LIMIT),
        name="dilated_attention",
    )(view(aq), view(ak), view(av), bias, norm_w.reshape(ATT_PAIRS, 1, LANES))
    return out.reshape(ATT_PAIRS, batch * rows, MAX_DIL * LANES)


def _scan_lanes(x, op, fill):
    n = x.shape[1]
    lane = lax.broadcasted_iota(jnp.int32, x.shape, 1)
    sh = 1
    while sh < n:
        x = op(x, jnp.where(lane >= sh, pltpu.roll(x, sh, axis=1), fill))
        sh *= 2
    return x


def _two_term_rows(row):
    r = lax.broadcasted_iota(jnp.int32, (2 * SUBLANES, row.shape[1]), 0)
    full = jnp.broadcast_to(row, r.shape)
    hi = full.astype(BF16).astype(F32)
    return jnp.where(r == 0, hi, jnp.where(r == 1, full - hi, 0.0)).astype(BF16)


def _mlstm_kernel(q_ref, k_ref, v_ref, o_ref, g_ref, nw_ref, h_ref, c_ref, n_ref, m_ref, sc_ref, et_ref):
    Lc, D, H = MLSTM_KERNEL_CHUNK, MLSTM_HEAD_DIM, MLSTM_HEADS
    nb, ts = q_ref.shape[1], q_ref.shape[3]

    @pl.when(pl.program_id(1) == 0)
    def _():
        c_ref[...] = jnp.zeros(c_ref.shape, F32)
        n_ref[...] = jnp.zeros(n_ref.shape, F32)
        m_ref[...] = jnp.zeros(m_ref.shape, F32)

    si = lax.broadcasted_iota(jnp.int32, (Lc, Lc), 0)
    ti = lax.broadcasted_iota(jnp.int32, (Lc, Lc), 1)
    causal = si <= ti
    nc = ts // Lc

    tiles = [(c, b) for c in range(nc) for b in range(nb)]
    log_i = jnp.concatenate([g_ref[b, 0, :, Lc * c:Lc * (c + 1)] for c, b in tiles], axis=0)
    log_f = jnp.concatenate([g_ref[b, 1, :, Lc * c:Lc * (c + 1)] for c, b in tiles], axis=0)
    bcum = _scan_lanes(log_f, jnp.add, 0.0)
    e = log_i - bcum
    sc_ref[0] = bcum
    sc_ref[1] = e
    sc_ref[2] = _scan_lanes(e, jnp.maximum, NEG)
    rows_c = nb * SUBLANES
    for c in range(nc):
        e_c = e[rows_c * c:rows_c * (c + 1)]
        et_ref[c] = jnp.transpose(jnp.concatenate([e_c, jnp.zeros((Lc - rows_c, Lc), F32)], axis=0))

    def chunk(c, carry):
        t0 = pl.multiple_of(c * Lc, Lc)
        e_cols = et_ref[c]
        heads = [(b, h) for b in range(nb) for h in range(H)]
        rows = {}
        for b in range(nb):
            r0 = pl.multiple_of((c * nb + b) * SUBLANES, SUBLANES)
            bcum = sc_ref[0, pl.ds(r0, SUBLANES), :]
            e = sc_ref[1, pl.ds(r0, SUBLANES), :]
            m_prev = m_ref[b]
            g = bcum[:, Lc - 1:Lc]
            m_t = bcum + jnp.maximum(m_prev, sc_ref[2, pl.ds(r0, SUBLANES), :])
            c_minus_m = bcum - m_t
            a = g + e
            m_new = jnp.maximum(g + m_prev, jnp.max(a, axis=1, keepdims=True))
            rows[b] = dict(c_minus_m=c_minus_m, inter=jnp.exp(c_minus_m + m_prev), floor=jnp.exp(-m_t),
                           decay=jnp.exp(g + m_prev - m_new), w=jnp.exp(a - m_new))
            m_ref[b] = jnp.broadcast_to(m_new, m_prev.shape)

        qt = {bh: q_ref[bh[1], bh[0], :, pl.ds(t0, Lc)] for bh in heads}
        k = {bh: k_ref[bh[1], bh[0], pl.ds(t0, Lc), :] for bh in heads}
        vt = {bh: v_ref[bh[1], bh[0], :, pl.ds(t0, Lc)] for bh in heads}
        st = {bh: jnp.dot(k[bh], qt[bh], preferred_element_type=F32) for bh in heads}
        cq = {(b, h): jnp.dot(c_ref[b * H + h].astype(BF16), qt[b, h], preferred_element_type=F32) for b, h in heads}
        nq = {(b, h): jnp.dot(_two_term_rows(n_ref[b * H + h, 0:1, :]), qt[b, h], preferred_element_type=F32)
              for b, h in heads}
        scores = {}
        for b, h in heads:
            col = SUBLANES * b + h
            dprime = jnp.where(causal, e_cols[:, col:col + 1] + rows[b]["c_minus_m"][h:h + 1, :], NEG)
            scores[b, h] = st[b, h] * jnp.exp(dprime)
        num = {(b, h): jnp.dot(vt[b, h], scores[b, h].astype(BF16), preferred_element_type=F32)
               + rows[b]["inter"][h:h + 1, :] * cq[b, h] for b, h in heads}
        for b, h in heads:
            inter = rows[b]["inter"][h:h + 1, :]
            den = jnp.sum(scores[b, h], axis=0, keepdims=True) + inter * (nq[b, h][0:1, :] + nq[b, h][1:2, :])
            hh = num[b, h] * (1.0 / jnp.maximum(jnp.abs(den), rows[b]["floor"][h:h + 1, :]))
            hh = hh * lax.rsqrt(jnp.mean(hh * hh, axis=0, keepdims=True) + EPS) * nw_ref[h]
            gate = jax.nn.sigmoid(o_ref[h, b, :, pl.ds(t0, Lc)].astype(F32))
            h_ref[h, b, :, pl.ds(t0, Lc)] = (gate * hh).astype(BF16)
        for b, h in heads:
            s = b * H + h
            decay = rows[b]["decay"][h:h + 1, :]
            w_row = rows[b]["w"][h:h + 1, :]
            dc = jnp.dot((vt[b, h].astype(F32) * w_row).astype(BF16), k[b, h], preferred_element_type=F32)
            c_ref[s] = decay * c_ref[s] + dc
            dn = jnp.dot(_two_term_rows(w_row), k[b, h], preferred_element_type=F32)
            n_ref[s, 0:1, :] = decay * n_ref[s, 0:1, :] + dn[0:1, :] + dn[1:2, :]
        return carry

    lax.fori_loop(0, ts // Lc, chunk, 0)


def _mlstm(mqt, mk, mvt, mot, gates, norm_w, batch, seq_len):
    H, D = MLSTM_HEADS, MLSTM_HEAD_DIM
    ts = MLSTM_TILE
    nb = MLSTM_BATCHES if batch % MLSTM_BATCHES == 0 else 1
    assert seq_len % ts == 0 and ts % MLSTM_KERNEL_CHUNK == 0 and D == MLSTM_KERNEL_CHUNK
    nt = seq_len // ts
    tspec = pl.BlockSpec((H, nb, D, ts), lambda b, j: (0, b, 0, j))
    nw_cols = jnp.broadcast_to(norm_w.reshape(H, D, 1), (H, D, LANES))
    return pl.pallas_call(
        _mlstm_kernel,
        grid=(batch // nb, nt),
        in_specs=[tspec,
                  pl.BlockSpec((H, nb, ts, D), lambda b, j: (0, b, j, 0)),
                  tspec, tspec,
                  pl.BlockSpec((nb, 2, SUBLANES, ts), lambda b, j: (b, 0, 0, j)),
                  pl.BlockSpec((H, D, LANES), lambda b, j: (0, 0, 0))],
        out_specs=tspec,
        out_shape=jax.ShapeDtypeStruct((H, batch, D, seq_len), BF16),
        scratch_shapes=[pltpu.VMEM((nb * H, D, D), F32), pltpu.VMEM((nb * H, HALO, D), F32),
                        pltpu.VMEM((nb, SUBLANES, LANES), F32),
                        pltpu.VMEM((3, (ts // D) * nb * SUBLANES, D), F32), pltpu.VMEM((ts // D, D, D), F32)],
        compiler_params=pltpu.CompilerParams(
            dimension_semantics=("arbitrary", "arbitrary"), vmem_limit_bytes=VMEM_LIMIT),
        name="mlstm",
    )(mqt, mk.reshape(H, batch, seq_len, D), mvt, mot, gates, nw_cols)


def _ffn_kernel(x_ref, att_ref, hm_ref, permt_ref, wout_ref, fnw_ref, wup_ref, cw_ref, cb_ref, wdown_ref, finw_ref,
                o_ref, us_ref, *, tiles_per_seq, final_norm):
    tm = x_ref.shape[0]
    fc = FFN_CHUNK
    d_ff = wdown_ref.shape[0]
    nf = d_ff // fc
    i = pl.program_id(0)

    att = jnp.concatenate(
        [jnp.concatenate([att_ref[p, :, LANES * r:LANES * (r + 1)] for r in range(MAX_DIL)], axis=0)
         for p in range(ATT_PAIRS)], axis=1)
    att = jnp.dot(permt_ref[...], att, preferred_element_type=F32).astype(BF16)
    hm = [jnp.transpose(hm_ref[h].astype(F32)).astype(BF16) for h in range(MLSTM_HEADS)]
    y = jnp.concatenate([att] + hm, axis=1)
    h1 = x_ref[...] + jnp.dot(y, wout_ref[...], preferred_element_type=F32)
    o_ref[...] = h1
    xn = _rms(h1, fnw_ref[...]).astype(BF16)

    @pl.when(i % tiles_per_seq == 0)
    def _():
        us_ref[0:HALO, :] = jnp.zeros((HALO, us_ref.shape[1]), F32)

    us_ref[HALO:HALO + tm, :] = jnp.dot(xn, wup_ref[...], preferred_element_type=F32)

    def conv(c0):
        u = cb_ref[:, c0:c0 + fc]
        for j in range(FFN_CONV):
            off = HALO - (FFN_CONV - 1) + j
            u = u + us_ref[off:off + tm, c0:c0 + fc] * cw_ref[j:j + 1, c0:c0 + fc]
        return u

    acts = []
    for c in range(nf):
        gate, val = conv(c * fc), conv(d_ff + c * fc)
        acts.append((gate / (1.0 + jnp.exp(-gate)) * val).astype(BF16))
    us_ref[0:HALO, :] = us_ref[tm:tm + HALO, :]
    o_ref[...] += jnp.dot(jnp.concatenate(acts, axis=1), wdown_ref[...], preferred_element_type=F32)
    if final_norm:
        o_ref[...] = _rms(o_ref[...], finw_ref[...])


def _out_ffn(x2d, att, hm, w_out, ffn_norm_w, w_up, conv_w, conv_b, w_down, final_w, seq_len, final_norm):
    T, D = x2d.shape
    tm = FFN_TILE
    assert T % tm == 0 and seq_len % tm == 0
    d_ff = w_down.shape[0]
    assert d_ff % FFN_CHUNK == 0
    tps = seq_len // tm
    row = lambda i: (i, 0)
    const2 = lambda i: (0, 0)
    blk3 = lambda i: (0, i, 0)
    once = dict(pipeline_mode=pl.Buffered(1))
    assert tm == PERM_TILE
    permt = jnp.asarray(_dilation_permutation(PERM_TILE).T, BF16)
    return pl.pallas_call(
        functools.partial(_ffn_kernel, tiles_per_seq=seq_len // tm, final_norm=final_norm),
        grid=(T // tm,),
        in_specs=[
            pl.BlockSpec((tm, D), row),
            pl.BlockSpec((ATT_PAIRS, tm // MAX_DIL, MAX_DIL * LANES), blk3),
            pl.BlockSpec((MLSTM_HEADS, None, MLSTM_HEAD_DIM, tm), lambda i: (0, i // tps, 0, i % tps)),
            pl.BlockSpec(permt.shape, const2),
            pl.BlockSpec(w_out.shape, const2, **once),
            pl.BlockSpec((1, D), const2),
            pl.BlockSpec(w_up.shape, const2, **once),
            pl.BlockSpec(conv_w.shape, const2),
            pl.BlockSpec(conv_b.shape, const2),
            pl.BlockSpec(w_down.shape, const2, **once),
            pl.BlockSpec((1, D), const2),
        ],
        out_specs=pl.BlockSpec((tm, D), row),
        out_shape=jax.ShapeDtypeStruct((T, D), F32),
        scratch_shapes=[pltpu.VMEM((HALO + tm, 2 * d_ff), F32)],
        compiler_params=pltpu.CompilerParams(
            dimension_semantics=("arbitrary",), vmem_limit_bytes=VMEM_LIMIT),
        name="out_proj_conv_ffn",
    )(x2d, att, hm, permt, w_out, ffn_norm_w, w_up, conv_w, conv_b, w_down, final_w)


def kernel(x, w_in, mlstm_conv_w, mlstm_conv_b, mlstm_i_bias, mlstm_f_bias, att_out_norm_w, mlstm_out_norm_w, w_out, mixer_norm_w, ffn_norm_w, w_ffn_up, ffn_conv_w, ffn_conv_b, w_ffn_down, final_norm_w):
    batch, seq_len, d_model = x.shape
    depth = w_in.shape[0]
    h = x.reshape(batch * seq_len, d_model)
    for layer in range(depth):
        w = w_in[layer]
        n_nat = 3 * ATT_WIDTH + 2 * MLSTM_WIDTH
        n_gate = n_nat + 2 * MLSTM_WIDTH
        H = MLSTM_HEADS
        gate_cols = jnp.zeros((d_model, LANES), w.dtype)
        gate_cols = gate_cols.at[:, 0:H].set(w[:, n_gate:n_gate + H]).at[:, SUBLANES:SUBLANES + H].set(w[:, n_gate + H:])
        w_nat = jnp.concatenate([w[:, :n_nat], gate_cols], axis=1).astype(BF16)
        w_t = w[:, n_nat:n_gate].T.astype(BF16)
        gate_b = jnp.zeros((1, LANES), F32)
        gate_b = gate_b.at[0, 0:H].set(mlstm_i_bias[layer]).at[0, SUBLANES:SUBLANES + H].set(mlstm_f_bias[layer])
        aq, ak, av, mq, mk, mv, mo, gates = _input_projection(
            h, mixer_norm_w[layer][None, :], w_nat, w_t, mlstm_conv_w[layer],
            mlstm_conv_b[layer][None, :], gate_b, batch, seq_len)
        att = _dilated_attention(aq, ak, av, att_out_norm_w[layer], batch, seq_len)
        hm = _mlstm(mq, mk, mv, mo, gates, mlstm_out_norm_w[layer], batch, seq_len)
        h = _out_ffn(h, att, hm, w_out[layer].astype(BF16), ffn_norm_w[layer][None, :], w_ffn_up[layer].astype(BF16),
                     ffn_conv_w[layer], ffn_conv_b[layer][None, :], w_ffn_down[layer].astype(BF16),
                     final_norm_w[None, :], seq_len, final_norm=(layer == depth - 1))
    return h.reshape(batch, seq_len, d_model)
```

```python
import functools

import numpy as np
import jax
import jax.numpy as jnp
from jax import lax
from jax.experimental import pallas as pl
from jax.experimental.pallas import tpu as pltpu

F32 = jnp.float32
BF16 = jnp.bfloat16

EPS = 1e-6
LANES = 128
ATT_HEADS = 8
ATT_HEAD_DIM = 64
ATT_WIDTH = ATT_HEADS * ATT_HEAD_DIM
ATT_PAIRS = ATT_WIDTH // LANES
ATT_BLOCK = 128
DILATIONS = (16, 4, 1)
MAX_DIL = 16
ATT_TILE = ATT_BLOCK * MAX_DIL
MLSTM_HEADS = 4
MLSTM_HEAD_DIM = 128
MLSTM_WIDTH = MLSTM_HEADS * MLSTM_HEAD_DIM
MLSTM_KERNEL_CHUNK = 128
MLSTM_CONV = 4
FFN_CONV = 3
SUBLANES = 8
HALO = SUBLANES
NEG = -1e30
LOG2E = 1.4426950408889634
VMEM_LIMIT = 56 * 1024 * 1024

PROJ_TILE = 1024
PERM_TILE = 256
FFN_TILE = 512
FFN_CHUNK = 256
MLSTM_TILE = 512
MLSTM_BATCHES = 8


def _rms(x, w):
    return x * lax.rsqrt(jnp.mean(x * x, axis=-1, keepdims=True) + EPS) * w


def _dilation_permutation(tile):
    rows = tile // MAX_DIL
    n = np.arange(tile)
    p = np.zeros((tile, tile), np.float32)
    p[n, MAX_DIL * (n % rows) + n // rows] = 1.0
    return p


def _proj_kernel(x_ref, nw_ref, w_ref, perm_ref, cw_ref, cb_ref, gb_ref,
                 aq_ref, ak_ref, av_ref, mq_ref, mk_ref, mv_ref, mo_ref, g_ref,
                 cs_ref, vo_ref, *, tiles_per_seq):
    tm = x_ref.shape[0]
    i = pl.program_id(0)
    xn = _rms(x_ref[...], nw_ref[...]).astype(BF16)
    xp = jnp.concatenate(
        [jnp.dot(perm_ref[...], xn[PERM_TILE * g:PERM_TILE * (g + 1)], preferred_element_type=F32).astype(BF16)
         for g in range(tm // PERM_TILE)], axis=0)
    base = 3 * ATT_WIDTH

    def seg(lhs, c0, width):
        return lambda: jnp.dot(lhs, w_ref[:, c0:c0 + width], preferred_element_type=F32)

    def put_view(ref, y):
        rows = PERM_TILE // MAX_DIL
        for g in range(tm // PERM_TILE):
            for p in range(ATT_PAIRS):
                for r in range(MAX_DIL):
                    src = PERM_TILE * g + rows * r
                    ref[p, rows * g:rows * (g + 1), LANES * r:LANES * (r + 1)] = (
                        y[src:src + rows, LANES * p:LANES * (p + 1)].astype(BF16))

    @pl.when(i % tiles_per_seq == 0)
    def _():
        cs_ref[:, HALO:2 * HALO, :] = jnp.zeros((MLSTM_CONV, HALO, cs_ref.shape[2]), F32)

    def conv_silu_block(p):
        cols = slice(LANES * p, LANES * (p + 1))
        y = cb_ref[:, cols]
        for j in range(MLSTM_CONV):
            y = y + cs_ref[j, HALO:HALO + tm, cols] * cw_ref[j:j + 1, cols]
        y = y / (1.0 + jnp.exp(-y))
        if p < MLSTM_HEADS:
            mq_ref[p] = jnp.transpose(y).astype(BF16)
        else:
            mk_ref[p - MLSTM_HEADS] = (y * (MLSTM_HEAD_DIM ** -0.5)).astype(BF16)

    put_view(aq_ref, seg(xp, 0, ATT_WIDTH)() * (ATT_HEAD_DIM ** -0.5 * LOG2E))
    put_view(ak_ref, seg(xp, ATT_WIDTH, ATT_WIDTH)())
    put_view(av_ref, seg(xp, 2 * ATT_WIDTH, ATT_WIDTH)())

    y = seg(xn, base, 2 * MLSTM_WIDTH)()
    for j in range(MLSTM_CONV):
        sh = MLSTM_CONV - 1 - j
        cs_ref[j, HALO + sh:HALO + sh + tm, :] = y

    pieces = 2 * MLSTM_HEADS // 2
    for piece in range(pieces):
        c0 = base + 2 * MLSTM_WIDTH + 2 * LANES * piece
        vo_ref[...] = seg(xn, c0, 2 * LANES)()
        for half in range(2):
            head = 2 * piece + half
            dst = mv_ref if head < MLSTM_HEADS else mo_ref
            dst[head % MLSTM_HEADS] = jnp.transpose(vo_ref[:, LANES * half:LANES * (half + 1)]).astype(BF16)
        conv_silu_block(2 * piece)
        conv_silu_block(2 * piece + 1)
    for j in range(MLSTM_CONV):
        cs_ref[j, HALO:2 * HALO, :] = cs_ref[j, HALO + tm:2 * HALO + tm, :]

    z = jnp.transpose(seg(xn, base + 4 * MLSTM_WIDTH, LANES)() + gb_ref[...])
    g_ref[0] = z[0:SUBLANES]
    zf = z[SUBLANES:2 * SUBLANES]
    g_ref[1] = jnp.minimum(zf, 0.0) - jnp.log1p(jnp.exp(-jnp.abs(zf)))


def _input_projection(x2d, norm_w, w_pad, conv_w, conv_b, gate_b, batch, seq_len):
    T, D = x2d.shape
    tm = PROJ_TILE
    assert T % tm == 0 and seq_len % tm == 0 and tm % PERM_TILE == 0
    tps = seq_len // tm
    row = lambda i: (i, 0)
    const2 = lambda i: (0, 0)
    blk3 = lambda i: (0, i, 0)
    perm = jnp.asarray(_dilation_permutation(PERM_TILE), BF16)
    H, Dh = MLSTM_HEADS, MLSTM_HEAD_DIM
    heads_shape = jax.ShapeDtypeStruct((H, T, Dh), BF16)
    heads_spec = pl.BlockSpec((H, tm, Dh), blk3)
    heads_t_shape = jax.ShapeDtypeStruct((H, batch, Dh, seq_len), BF16)
    heads_t_spec = pl.BlockSpec((H, None, Dh, tm), lambda i: (0, i // tps, 0, i % tps))
    view_shape = jax.ShapeDtypeStruct((ATT_PAIRS, T // MAX_DIL, MAX_DIL * LANES), BF16)
    view_spec = pl.BlockSpec((ATT_PAIRS, tm // MAX_DIL, MAX_DIL * LANES), blk3)
    gate_shape = jax.ShapeDtypeStruct((batch, 2, SUBLANES, seq_len), F32)
    gate_spec = pl.BlockSpec((None, 2, SUBLANES, tm), lambda i: (i // tps, 0, 0, i % tps))
    return pl.pallas_call(
        functools.partial(_proj_kernel, tiles_per_seq=tps),
        grid=(T // tm,),
        in_specs=[
            pl.BlockSpec((tm, D), row),
            pl.BlockSpec((1, D), const2),
            pl.BlockSpec(w_pad.shape, const2),
            pl.BlockSpec(perm.shape, const2),
            pl.BlockSpec(conv_w.shape, const2),
            pl.BlockSpec(conv_b.shape, const2),
            pl.BlockSpec(gate_b.shape, const2),
        ],
        out_specs=[view_spec] * 3 + [heads_t_spec, heads_spec, heads_t_spec, heads_t_spec, gate_spec],
        out_shape=[view_shape] * 3 + [heads_t_shape, heads_shape, heads_t_shape, heads_t_shape, gate_shape],
        scratch_shapes=[pltpu.VMEM((MLSTM_CONV, 3 * HALO + tm, 2 * MLSTM_WIDTH), F32),
                        pltpu.VMEM((tm, 2 * LANES), F32)],
        compiler_params=pltpu.CompilerParams(
            dimension_semantics=("arbitrary",), vmem_limit_bytes=VMEM_LIMIT),
        name="input_projection",
    )(x2d, norm_w, w_pad, perm, conv_w, conv_b, gate_b)


def _attention_biases():
    n = np.arange(ATT_BLOCK)
    s = np.arange(2 * ATT_BLOCK)
    out = np.zeros((3, 2, 2 * ATT_BLOCK, 2 * ATT_BLOCK), np.float32)
    pq16, pk16 = n, s
    pq4 = 4 * (n % 32) + n // 32
    pk4 = 4 * (s % 64) + s // 64
    perm1 = 16 * (n % 8) + n // 8
    pq1 = perm1
    pk1 = perm1[s % ATT_BLOCK] + ATT_BLOCK * (s // ATT_BLOCK)
    shifts = (ATT_BLOCK, 4 * 32, ATT_BLOCK)
    for g, (pq, pk, shift) in enumerate(((pq16, pk16, shifts[0]), (pq4, pk4, shifts[1]), (pq1, pk1, shifts[2]))):
        for variant, sh in enumerate((shift, 0)):
            dist = pq[:, None] - (pk[None, :] - sh)
            ok = (dist >= 0) & (dist <= ATT_BLOCK)
            bias = np.where(ok, 0.0, NEG).astype(np.float32)
            out[g, variant] = np.concatenate([bias, bias], axis=0)
    return out


def _attn_unit(q, kk, vv, bias, half0):
    zero = jnp.zeros_like(q)
    q2 = jnp.concatenate([jnp.where(half0, q, zero), jnp.where(half0, zero, q)], axis=0)
    s = lax.dot_general(q2, kk, (((1,), (1,)), ((), ())), preferred_element_type=F32) + bias
    m = jnp.max(s, axis=-1, keepdims=True)
    p = jnp.exp2(s - m).astype(BF16)
    vext = jnp.concatenate([vv, jnp.ones_like(vv)], axis=1)
    res = jnp.dot(p, vext, preferred_element_type=F32)
    top, bot = res[:ATT_BLOCK], res[ATT_BLOCK:]
    o = jnp.where(half0, top[:, :LANES], bot[:, :LANES])
    den = jnp.where(half0, top[:, LANES:], bot[:, LANES:])
    mb = jnp.where(half0, m[:ATT_BLOCK], m[ATT_BLOCK:])
    return o, mb, den


def _attention_kernel(q_ref, k_ref, v_ref, bias_ref, nw_ref, o_ref, qf_ref, kf_ref, vf_ref, st_ref):
    jt = pl.program_id(2)
    half0 = lax.broadcasted_iota(jnp.int32, (ATT_BLOCK, LANES), 1) < ATT_HEAD_DIM
    row0 = pl.multiple_of(jt * ATT_BLOCK, ATT_BLOCK)

    def lanes(r):
        return slice(LANES * r, LANES * (r + 1))

    @pl.when(jt == 0)
    def _():
        qf_ref[...] = q_ref[...].astype(F32)
        kf_ref[...] = k_ref[...].astype(F32)
        vf_ref[...] = v_ref[...].astype(F32)

    def store_stats(g, r, rows, stats):
        for st, val in enumerate(stats):
            st_ref[g, st, r, rows, :] = val

    first16 = jt == 0
    k0 = pl.multiple_of(jnp.maximum(row0 - ATT_BLOCK, 0), ATT_BLOCK)
    bias16 = bias_ref[0, jnp.where(first16, 1, 0)]
    for r in range(MAX_DIL):
        q = q_ref[pl.ds(row0, ATT_BLOCK), lanes(r)]
        kk = k_ref[pl.ds(k0, 2 * ATT_BLOCK), lanes(r)]
        vv = v_ref[pl.ds(k0, 2 * ATT_BLOCK), lanes(r)]
        store_stats(0, r, slice(None), _attn_unit(q, kk, vv, bias16, half0))

    def body4(bb, carry):
        gb = jt * 4 + bb
        q0 = pl.multiple_of(gb * 32, 32)
        k0 = pl.multiple_of(jnp.maximum(gb * 32 - 32, 0), 32)
        bias = bias_ref[1, jnp.where(gb == 0, 1, 0)]
        o0 = pl.multiple_of(bb * 32, 32)
        for r0 in range(4):
            q = jnp.concatenate([q_ref[pl.ds(q0, 32), lanes(4 * r1 + r0)] for r1 in range(4)], axis=0)
            kk = jnp.concatenate([k_ref[pl.ds(k0, 64), lanes(4 * r1 + r0)] for r1 in range(4)], axis=0)
            vv = jnp.concatenate([v_ref[pl.ds(k0, 64), lanes(4 * r1 + r0)] for r1 in range(4)], axis=0)
            stats = _attn_unit(q, kk, vv, bias, half0)
            for r1 in range(4):
                store_stats(1, 4 * r1 + r0, pl.ds(o0, 32), [s[32 * r1:32 * (r1 + 1)] for s in stats])
        return carry

    lax.fori_loop(0, 4, body4, 0, unroll=True)

    def body1(bl, carry):
        b = jt * MAX_DIL + bl
        q0 = pl.multiple_of(b * 8, 8)
        k0 = pl.multiple_of(jnp.maximum(b * 8 - 8, 0), 8)
        bias = bias_ref[2, jnp.where(b == 0, 1, 0)]
        o0 = pl.multiple_of(bl * 8, 8)

        def gather(ref, start):
            return jnp.concatenate([ref[pl.ds(start, 8), lanes(r)] for r in range(MAX_DIL)], axis=0)

        q = gather(qf_ref, q0).astype(BF16)
        kk = jnp.concatenate([gather(kf_ref, k0), gather(kf_ref, k0 + 8)], axis=0).astype(BF16)
        vv = jnp.concatenate([gather(vf_ref, k0), gather(vf_ref, k0 + 8)], axis=0).astype(BF16)
        stats = _attn_unit(q, kk, vv, bias, half0)
        for r in range(MAX_DIL):
            store_stats(2, r, pl.ds(o0, 8), [s[8 * r:8 * (r + 1)] for s in stats])
        return carry

    lax.fori_loop(0, MAX_DIL, body1, 0, unroll=True)

    nw = nw_ref[...]
    for r in range(MAX_DIL):
        ms = [st_ref[g, 1, r] for g in range(3)]
        mx = jnp.maximum(jnp.maximum(ms[0], ms[1]), ms[2])
        fs = [jnp.exp2(m - mx) for m in ms]
        num = fs[0] * st_ref[0, 0, r] + fs[1] * st_ref[1, 0, r] + fs[2] * st_ref[2, 0, r]
        den = fs[0] * st_ref[0, 2, r] + fs[1] * st_ref[1, 2, r] + fs[2] * st_ref[2, 2, r]
        att = num / den
        a2 = att * att
        s0 = jnp.sum(jnp.where(half0, a2, 0.0), axis=-1, keepdims=True)
        s1 = jnp.sum(jnp.where(half0, 0.0, a2), axis=-1, keepdims=True)
        msq = jnp.where(half0, s0, s1) * (1.0 / ATT_HEAD_DIM)
        o_ref[pl.ds(row0, ATT_BLOCK), lanes(r)] = (att * lax.rsqrt(msq + EPS) * nw).astype(BF16)


def _dilated_attention(aq, ak, av, norm_w, batch, seq_len):
    assert seq_len % ATT_TILE == 0 and seq_len >= 2 * ATT_TILE
    rows = seq_len // MAX_DIL
    nt = seq_len // ATT_TILE
    view = lambda t: t.reshape(ATT_PAIRS, batch, rows, MAX_DIL * LANES)
    bias = jnp.asarray(_attention_biases())
    seq_spec = pl.BlockSpec((None, None, rows, MAX_DIL * LANES), lambda b, p, j: (p, b, 0, 0))
    out = pl.pallas_call(
        _attention_kernel,
        grid=(batch, ATT_PAIRS, nt),
        in_specs=[seq_spec, seq_spec, seq_spec,
                  pl.BlockSpec(bias.shape, lambda b, p, j: (0, 0, 0, 0)),
                  pl.BlockSpec((None, 1, LANES), lambda b, p, j: (p, 0, 0))],
        out_specs=seq_spec,
        out_shape=jax.ShapeDtypeStruct((ATT_PAIRS, batch, rows, MAX_DIL * LANES), BF16),
        scratch_shapes=[pltpu.VMEM((rows, MAX_DIL * LANES), F32)] * 3
        + [pltpu.VMEM((3, 3, MAX_DIL, ATT_BLOCK, LANES), F32)],
        compiler_params=pltpu.CompilerParams(
            dimension_semantics=("arbitrary", "arbitrary", "arbitrary"), vmem_limit_bytes=VMEM_LIMIT),
        name="dilated_attention",
    )(view(aq), view(ak), view(av), bias, norm_w.reshape(ATT_PAIRS, 1, LANES))
    return out.reshape(ATT_PAIRS, batch * rows, MAX_DIL * LANES)


def _scan_lanes(x, op, fill):
    n = x.shape[1]
    lane = lax.broadcasted_iota(jnp.int32, x.shape, 1)
    sh = 1
    while sh < n:
        x = op(x, jnp.where(lane >= sh, pltpu.roll(x, sh, axis=1), fill))
        sh *= 2
    return x


def _two_term_rows(row):
    r = lax.broadcasted_iota(jnp.int32, (2 * SUBLANES, row.shape[1]), 0)
    full = jnp.broadcast_to(row, r.shape)
    hi = full.astype(BF16).astype(F32)
    return jnp.where(r == 0, hi, jnp.where(r == 1, full - hi, 0.0)).astype(BF16)


def _mlstm_kernel(q_ref, k_ref, v_ref, o_ref, g_ref, nw_ref, h_ref, c_ref, n_ref, m_ref, sc_ref, et_ref):
    Lc, D, H = MLSTM_KERNEL_CHUNK, MLSTM_HEAD_DIM, MLSTM_HEADS
    nb, ts = q_ref.shape[1], q_ref.shape[3]

    @pl.when(pl.program_id(1) == 0)
    def _():
        c_ref[...] = jnp.zeros(c_ref.shape, F32)
        n_ref[...] = jnp.zeros(n_ref.shape, F32)
        m_ref[...] = jnp.zeros(m_ref.shape, F32)

    si = lax.broadcasted_iota(jnp.int32, (Lc, Lc), 0)
    ti = lax.broadcasted_iota(jnp.int32, (Lc, Lc), 1)
    causal = si <= ti
    nc = ts // Lc

    tiles = [(c, b) for c in range(nc) for b in range(nb)]
    log_i = jnp.concatenate([g_ref[b, 0, :, Lc * c:Lc * (c + 1)] for c, b in tiles], axis=0)
    log_f = jnp.concatenate([g_ref[b, 1, :, Lc * c:Lc * (c + 1)] for c, b in tiles], axis=0)
    bcum = _scan_lanes(log_f, jnp.add, 0.0)
    e = log_i - bcum
    sc_ref[0] = bcum
    sc_ref[1] = e
    sc_ref[2] = _scan_lanes(e, jnp.maximum, NEG)
    rows_c = nb * SUBLANES
    for c in range(nc):
        e_c = e[rows_c * c:rows_c * (c + 1)]
        et_ref[c] = jnp.transpose(jnp.concatenate([e_c, jnp.zeros((Lc - rows_c, Lc), F32)], axis=0))

    def chunk(c, carry):
        t0 = pl.multiple_of(c * Lc, Lc)
        e_cols = et_ref[c]
        heads = [(b, h) for b in range(nb) for h in range(H)]
        rows = {}
        for b in range(nb):
            r0 = pl.multiple_of((c * nb + b) * SUBLANES, SUBLANES)
            bcum = sc_ref[0, pl.ds(r0, SUBLANES), :]
            e = sc_ref[1, pl.ds(r0, SUBLANES), :]
            m_prev = m_ref[b]
            g = bcum[:, Lc - 1:Lc]
            m_t = bcum + jnp.maximum(m_prev, sc_ref[2, pl.ds(r0, SUBLANES), :])
            c_minus_m = bcum - m_t
            a = g + e
            m_new = jnp.maximum(g + m_prev, jnp.max(a, axis=1, keepdims=True))
            rows[b] = dict(c_minus_m=c_minus_m, inter=jnp.exp(c_minus_m + m_prev), floor=jnp.exp(-m_t),
                           decay=jnp.exp(g + m_prev - m_new), w=jnp.exp(a - m_new))
            m_ref[b] = jnp.broadcast_to(m_new, m_prev.shape)

        qt = {bh: q_ref[bh[1], bh[0], :, pl.ds(t0, Lc)] for bh in heads}
        k = {bh: k_ref[bh[1], bh[0], pl.ds(t0, Lc), :] for bh in heads}
        vt = {bh: v_ref[bh[1], bh[0], :, pl.ds(t0, Lc)] for bh in heads}
        st = {bh: jnp.dot(k[bh], qt[bh], preferred_element_type=F32) for bh in heads}
        cq = {(b, h): jnp.dot(c_ref[b * H + h].astype(BF16), qt[b, h], preferred_element_type=F32) for b, h in heads}
        nq = {(b, h): jnp.dot(_two_term_rows(n_ref[b * H + h, 0:1, :]), qt[b, h], preferred_element_type=F32)
              for b, h in heads}
        scores = {}
        for b, h in heads:
            col = SUBLANES * b + h
            dprime = jnp.where(causal, e_cols[:, col:col + 1] + rows[b]["c_minus_m"][h:h + 1, :], NEG)
            scores[b, h] = st[b, h] * jnp.exp(dprime)
        num = {(b, h): jnp.dot(vt[b, h], scores[b, h].astype(BF16), preferred_element_type=F32)
               + rows[b]["inter"][h:h + 1, :] * cq[b, h] for b, h in heads}
        for b, h in heads:
            inter = rows[b]["inter"][h:h + 1, :]
            den = jnp.sum(scores[b, h], axis=0, keepdims=True) + inter * (nq[b, h][0:1, :] + nq[b, h][1:2, :])
            hh = num[b, h] * (1.0 / jnp.maximum(jnp.abs(den), rows[b]["floor"][h:h + 1, :]))
            hh = hh * lax.rsqrt(jnp.mean(hh * hh, axis=0, keepdims=True) + EPS) * nw_ref[h]
            gate = jax.nn.sigmoid(o_ref[h, b, :, pl.ds(t0, Lc)].astype(F32))
            h_ref[h, b, :, pl.ds(t0, Lc)] = (gate * hh).astype(BF16)
        for b, h in heads:
            s = b * H + h
            decay = rows[b]["decay"][h:h + 1, :]
            w_row = rows[b]["w"][h:h + 1, :]
            dc = jnp.dot((vt[b, h].astype(F32) * w_row).astype(BF16), k[b, h], preferred_element_type=F32)
            c_ref[s] = decay * c_ref[s] + dc
            dn = jnp.dot(_two_term_rows(w_row), k[b, h], preferred_element_type=F32)
            n_ref[s, 0:1, :] = decay * n_ref[s, 0:1, :] + dn[0:1, :] + dn[1:2, :]
        return carry

    lax.fori_loop(0, ts // Lc, chunk, 0, unroll=True)


def _mlstm(mqt, mk, mvt, mot, gates, norm_w, batch, seq_len):
    H, D = MLSTM_HEADS, MLSTM_HEAD_DIM
    ts = MLSTM_TILE
    nb = MLSTM_BATCHES if batch % MLSTM_BATCHES == 0 else 1
    assert seq_len % ts == 0 and ts % MLSTM_KERNEL_CHUNK == 0 and D == MLSTM_KERNEL_CHUNK
    nt = seq_len // ts
    tspec = pl.BlockSpec((H, nb, D, ts), lambda b, j: (0, b, 0, j))
    nw_cols = jnp.broadcast_to(norm_w.reshape(H, D, 1), (H, D, LANES))
    return pl.pallas_call(
        _mlstm_kernel,
        grid=(batch // nb, nt),
        in_specs=[tspec,
                  pl.BlockSpec((H, nb, ts, D), lambda b, j: (0, b, j, 0)),
                  tspec, tspec,
                  pl.BlockSpec((nb, 2, SUBLANES, ts), lambda b, j: (b, 0, 0, j)),
                  pl.BlockSpec((H, D, LANES), lambda b, j: (0, 0, 0))],
        out_specs=tspec,
        out_shape=jax.ShapeDtypeStruct((H, batch, D, seq_len), BF16),
        scratch_shapes=[pltpu.VMEM((nb * H, D, D), F32), pltpu.VMEM((nb * H, HALO, D), F32),
                        pltpu.VMEM((nb, SUBLANES, LANES), F32),
                        pltpu.VMEM((3, (ts // D) * nb * SUBLANES, D), F32), pltpu.VMEM((ts // D, D, D), F32)],
        compiler_params=pltpu.CompilerParams(
            dimension_semantics=("arbitrary", "arbitrary"), vmem_limit_bytes=VMEM_LIMIT),
        name="mlstm",
    )(mqt, mk.reshape(H, batch, seq_len, D), mvt, mot, gates, nw_cols)


def _ffn_kernel(x_ref, att_ref, hm_ref, permt_ref, wout_ref, fnw_ref, wup_ref, cw_ref, cb_ref, wdown_ref, finw_ref,
                o_ref, us_ref, *, tiles_per_seq, final_norm):
    tm = x_ref.shape[0]
    fc = FFN_CHUNK
    d_ff = wdown_ref.shape[0]
    nf = d_ff // fc
    i = pl.program_id(0)

    vrows = PERM_TILE // MAX_DIL
    att = jnp.concatenate(
        [jnp.dot(permt_ref[...],
                 jnp.concatenate(
                     [jnp.concatenate([att_ref[p, vrows * g:vrows * (g + 1), LANES * r:LANES * (r + 1)]
                                       for r in range(MAX_DIL)], axis=0) for p in range(ATT_PAIRS)], axis=1),
                 preferred_element_type=F32).astype(BF16)
         for g in range(tm // PERM_TILE)], axis=0)
    hm = [jnp.transpose(hm_ref[h].astype(F32)).astype(BF16) for h in range(MLSTM_HEADS)]
    y = jnp.concatenate([att] + hm, axis=1)
    h1 = x_ref[...] + jnp.dot(y, wout_ref[...], preferred_element_type=F32)
    o_ref[...] = h1
    xn = _rms(h1, fnw_ref[...]).astype(BF16)

    @pl.when(i % tiles_per_seq == 0)
    def _():
        us_ref[0:HALO, :] = jnp.zeros((HALO, us_ref.shape[1]), F32)

    us_ref[HALO:HALO + tm, :] = jnp.dot(xn, wup_ref[...], preferred_element_type=F32)

    def conv(c0):
        u = cb_ref[:, c0:c0 + fc]
        for j in range(FFN_CONV):
            off = HALO - (FFN_CONV - 1) + j
            u = u + us_ref[off:off + tm, c0:c0 + fc] * cw_ref[j:j + 1, c0:c0 + fc]
        return u

    acts = []
    for c in range(nf):
        gate, val = conv(c * fc), conv(d_ff + c * fc)
        acts.append((gate / (1.0 + jnp.exp(-gate)) * val).astype(BF16))
    us_ref[0:HALO, :] = us_ref[tm:tm + HALO, :]
    o_ref[...] += jnp.dot(jnp.concatenate(acts, axis=1), wdown_ref[...], preferred_element_type=F32)
    if final_norm:
        o_ref[...] = _rms(o_ref[...], finw_ref[...])


def _out_ffn(x2d, att, hm, w_out, ffn_norm_w, w_up, conv_w, conv_b, w_down, final_w, seq_len, final_norm):
    T, D = x2d.shape
    tm = FFN_TILE
    assert T % tm == 0 and seq_len % tm == 0
    d_ff = w_down.shape[0]
    assert d_ff % FFN_CHUNK == 0
    tps = seq_len // tm
    row = lambda i: (i, 0)
    const2 = lambda i: (0, 0)
    blk3 = lambda i: (0, i, 0)
    once = dict(pipeline_mode=pl.Buffered(1))
    assert tm % PERM_TILE == 0
    permt = jnp.asarray(_dilation_permutation(PERM_TILE).T, BF16)
    return pl.pallas_call(
        functools.partial(_ffn_kernel, tiles_per_seq=seq_len // tm, final_norm=final_norm),
        grid=(T // tm,),
        in_specs=[
            pl.BlockSpec((tm, D), row),
            pl.BlockSpec((ATT_PAIRS, tm // MAX_DIL, MAX_DIL * LANES), blk3),
            pl.BlockSpec((MLSTM_HEADS, None, MLSTM_HEAD_DIM, tm), lambda i: (0, i // tps, 0, i % tps)),
            pl.BlockSpec(permt.shape, const2),
            pl.BlockSpec(w_out.shape, const2, **once),
            pl.BlockSpec((1, D), const2),
            pl.BlockSpec(w_up.shape, const2, **once),
            pl.BlockSpec(conv_w.shape, const2),
            pl.BlockSpec(conv_b.shape, const2),
            pl.BlockSpec(w_down.shape, const2, **once),
            pl.BlockSpec((1, D), const2),
        ],
        out_specs=pl.BlockSpec((tm, D), row),
        out_shape=jax.ShapeDtypeStruct((T, D), F32),
        scratch_shapes=[pltpu.VMEM((HALO + tm, 2 * d_ff), F32)],
        compiler_params=pltpu.CompilerParams(
            dimension_semantics=("arbitrary",), vmem_limit_bytes=VMEM_LIMIT),
        name="out_proj_conv_ffn",
    )(x2d, att, hm, permt, w_out, ffn_norm_w, w_up, conv_w, conv_b, w_down, final_w)


def kernel(x, w_in, mlstm_conv_w, mlstm_conv_b, mlstm_i_bias, mlstm_f_bias, att_out_norm_w, mlstm_out_norm_w, w_out, mixer_norm_w, ffn_norm_w, w_ffn_up, ffn_conv_w, ffn_conv_b, w_ffn_down, final_norm_w):
    batch, seq_len, d_model = x.shape
    depth = w_in.shape[0]
    h = x.reshape(batch * seq_len, d_model)
    for layer in range(depth):
        w = w_in[layer]
        n_gate = 3 * ATT_WIDTH + 4 * MLSTM_WIDTH
        H = MLSTM_HEADS
        gate_cols = jnp.zeros((d_model, LANES), w.dtype)
        gate_cols = gate_cols.at[:, 0:H].set(w[:, n_gate:n_gate + H]).at[:, SUBLANES:SUBLANES + H].set(w[:, n_gate + H:])
        w_pad = jnp.concatenate([w[:, :n_gate], gate_cols], axis=1).astype(BF16)
        gate_b = jnp.zeros((1, LANES), F32)
        gate_b = gate_b.at[0, 0:H].set(mlstm_i_bias[layer]).at[0, SUBLANES:SUBLANES + H].set(mlstm_f_bias[layer])
        aq, ak, av, mq, mk, mv, mo, gates = _input_projection(
            h, mixer_norm_w[layer][None, :], w_pad, mlstm_conv_w[layer], mlstm_conv_b[layer][None, :],
            gate_b, batch, seq_len)
        att = _dilated_attention(aq, ak, av, att_out_norm_w[layer], batch, seq_len)
        hm = _mlstm(mq, mk, mv, mo, gates, mlstm_out_norm_w[layer], batch, seq_len)
        h = _out_ffn(h, att, hm, w_out[layer].astype(BF16), ffn_norm_w[layer][None, :], w_ffn_up[layer].astype(BF16),
                     ffn_conv_w[layer], ffn_conv_b[layer][None, :], w_ffn_down[layer].astype(BF16),
                     final_norm_w[None, :], seq_len, final_norm=(layer == depth - 1))
    return h.reshape(batch, seq_len, d_model)
```

```python
import functools

import numpy as np
import jax
import jax.numpy as jnp
from jax import lax
from jax.experimental import pallas as pl
from jax.experimental.pallas import tpu as pltpu

F32 = jnp.float32
BF16 = jnp.bfloat16

EPS = 1e-6
LANES = 128
ATT_HEADS = 8
ATT_HEAD_DIM = 64
ATT_WIDTH = ATT_HEADS * ATT_HEAD_DIM
ATT_PAIRS = ATT_WIDTH // LANES
ATT_BLOCK = 128
DILATIONS = (16, 4, 1)
MAX_DIL = 16
ATT_TILE = ATT_BLOCK * MAX_DIL
MLSTM_HEADS = 4
MLSTM_HEAD_DIM = 128
MLSTM_WIDTH = MLSTM_HEADS * MLSTM_HEAD_DIM
MLSTM_KERNEL_CHUNK = 128
MLSTM_CONV = 4
FFN_CONV = 3
SUBLANES = 8
HALO = SUBLANES
NEG = -1e30
LOG2E = 1.4426950408889634
VMEM_LIMIT = 56 * 1024 * 1024

PROJ_TILE = 1024
PERM_TILE = 256
FFN_TILE = 512
FFN_CHUNK = 256
MLSTM_TILE = 512
MLSTM_BATCHES = 8


def _rms(x, w):
    return x * lax.rsqrt(jnp.mean(x * x, axis=-1, keepdims=True) + EPS) * w


def _dilation_permutation(tile):
    rows = tile // MAX_DIL
    n = np.arange(tile)
    p = np.zeros((tile, tile), np.float32)
    p[n, MAX_DIL * (n % rows) + n // rows] = 1.0
    return p


def _proj_kernel(x_ref, nw_ref, w_ref, wg_ref, perm_ref, cw_ref, cb_ref, gb_ref,
                 aq_ref, ak_ref, av_ref, mq_ref, mk_ref, mv_ref, mo_ref, g_ref,
                 cs_ref, vo_ref, *, tiles_per_seq):
    tm = x_ref.shape[0]
    i = pl.program_id(0)
    xn = _rms(x_ref[...], nw_ref[...]).astype(BF16)
    xp = jnp.concatenate(
        [jnp.dot(perm_ref[...], xn[PERM_TILE * g:PERM_TILE * (g + 1)], preferred_element_type=F32).astype(BF16)
         for g in range(tm // PERM_TILE)], axis=0)
    base = 3 * ATT_WIDTH

    def seg(lhs, c0, width):
        return lambda: jnp.dot(lhs, w_ref[:, c0:c0 + width], preferred_element_type=F32)

    def put_view(ref, y):
        rows = PERM_TILE // MAX_DIL
        for g in range(tm // PERM_TILE):
            for p in range(ATT_PAIRS):
                for r in range(MAX_DIL):
                    src = PERM_TILE * g + rows * r
                    ref[p, rows * g:rows * (g + 1), LANES * r:LANES * (r + 1)] = (
                        y[src:src + rows, LANES * p:LANES * (p + 1)].astype(BF16))

    @pl.when(i % tiles_per_seq == 0)
    def _():
        cs_ref[:, HALO:2 * HALO, :] = jnp.zeros((MLSTM_CONV, HALO, cs_ref.shape[2]), F32)

    def conv_silu_block(p):
        cols = slice(LANES * p, LANES * (p + 1))
        y = cb_ref[:, cols]
        for j in range(MLSTM_CONV):
            y = y + cs_ref[j, HALO:HALO + tm, cols] * cw_ref[j:j + 1, cols]
        y = y / (1.0 + jnp.exp(-y))
        if p < MLSTM_HEADS:
            mq_ref[p] = jnp.transpose(y).astype(BF16)
        else:
            mk_ref[p - MLSTM_HEADS] = (y * (MLSTM_HEAD_DIM ** -0.5)).astype(BF16)

    put_view(aq_ref, seg(xp, 0, ATT_WIDTH)() * (ATT_HEAD_DIM ** -0.5 * LOG2E))
    put_view(ak_ref, seg(xp, ATT_WIDTH, ATT_WIDTH)())
    put_view(av_ref, seg(xp, 2 * ATT_WIDTH, ATT_WIDTH)())

    y = seg(xn, base, 2 * MLSTM_WIDTH)()
    for j in range(MLSTM_CONV):
        sh = MLSTM_CONV - 1 - j
        cs_ref[j, HALO + sh:HALO + sh + tm, :] = y

    pieces = 2 * MLSTM_HEADS // 2
    for piece in range(pieces):
        c0 = base + 2 * MLSTM_WIDTH + 2 * LANES * piece
        vo_ref[...] = seg(xn, c0, 2 * LANES)()
        for half in range(2):
            head = 2 * piece + half
            dst = mv_ref if head < MLSTM_HEADS else mo_ref
            dst[head % MLSTM_HEADS] = jnp.transpose(vo_ref[:, LANES * half:LANES * (half + 1)]).astype(BF16)
        conv_silu_block(2 * piece)
        conv_silu_block(2 * piece + 1)
    for j in range(MLSTM_CONV):
        cs_ref[j, HALO:2 * HALO, :] = cs_ref[j, HALO + tm:2 * HALO + tm, :]

    z = jnp.transpose(jnp.dot(xn, wg_ref[...], preferred_element_type=F32) + gb_ref[...])
    g_ref[0] = z[0:SUBLANES]
    zf = z[SUBLANES:2 * SUBLANES]
    g_ref[1] = jnp.minimum(zf, 0.0) - jnp.log1p(jnp.exp(-jnp.abs(zf)))


def _input_projection(x2d, norm_w, w_main, w_gate, conv_w, conv_b, gate_b, batch, seq_len):
    T, D = x2d.shape
    tm = PROJ_TILE
    assert T % tm == 0 and seq_len % tm == 0 and tm % PERM_TILE == 0
    tps = seq_len // tm
    row = lambda i: (i, 0)
    const2 = lambda i: (0, 0)
    blk3 = lambda i: (0, i, 0)
    perm = jnp.asarray(_dilation_permutation(PERM_TILE), BF16)
    H, Dh = MLSTM_HEADS, MLSTM_HEAD_DIM
    heads_shape = jax.ShapeDtypeStruct((H, T, Dh), BF16)
    heads_spec = pl.BlockSpec((H, tm, Dh), blk3)
    heads_t_shape = jax.ShapeDtypeStruct((H, batch, Dh, seq_len), BF16)
    heads_t_spec = pl.BlockSpec((H, None, Dh, tm), lambda i: (0, i // tps, 0, i % tps))
    view_shape = jax.ShapeDtypeStruct((ATT_PAIRS, T // MAX_DIL, MAX_DIL * LANES), BF16)
    view_spec = pl.BlockSpec((ATT_PAIRS, tm // MAX_DIL, MAX_DIL * LANES), blk3)
    gate_shape = jax.ShapeDtypeStruct((batch, 2, SUBLANES, seq_len), F32)
    gate_spec = pl.BlockSpec((None, 2, SUBLANES, tm), lambda i: (i // tps, 0, 0, i % tps))
    return pl.pallas_call(
        functools.partial(_proj_kernel, tiles_per_seq=tps),
        grid=(T // tm,),
        in_specs=[
            pl.BlockSpec((tm, D), row),
            pl.BlockSpec((1, D), const2),
            pl.BlockSpec(w_main.shape, const2),
            pl.BlockSpec(w_gate.shape, const2),
            pl.BlockSpec(perm.shape, const2),
            pl.BlockSpec(conv_w.shape, const2),
            pl.BlockSpec(conv_b.shape, const2),
            pl.BlockSpec(gate_b.shape, const2),
        ],
        out_specs=[view_spec] * 3 + [heads_t_spec, heads_spec, heads_t_spec, heads_t_spec, gate_spec],
        out_shape=[view_shape] * 3 + [heads_t_shape, heads_shape, heads_t_shape, heads_t_shape, gate_shape],
        scratch_shapes=[pltpu.VMEM((MLSTM_CONV, 3 * HALO + tm, 2 * MLSTM_WIDTH), F32),
                        pltpu.VMEM((tm, 2 * LANES), F32)],
        compiler_params=pltpu.CompilerParams(
            dimension_semantics=("arbitrary",), vmem_limit_bytes=VMEM_LIMIT),
        name="input_projection",
    )(x2d, norm_w, w_main, w_gate, perm, conv_w, conv_b, gate_b)


def _attention_biases():
    n = np.arange(ATT_BLOCK)
    s = np.arange(2 * ATT_BLOCK)
    out = np.zeros((3, 2, 2 * ATT_BLOCK, 2 * ATT_BLOCK), np.float32)
    pq16, pk16 = n, s
    pq4 = 4 * (n % 32) + n // 32
    pk4 = 4 * (s % 64) + s // 64
    perm1 = 16 * (n % 8) + n // 8
    pq1 = perm1
    pk1 = perm1[s % ATT_BLOCK] + ATT_BLOCK * (s // ATT_BLOCK)
    shifts = (ATT_BLOCK, 4 * 32, ATT_BLOCK)
    for g, (pq, pk, shift) in enumerate(((pq16, pk16, shifts[0]), (pq4, pk4, shifts[1]), (pq1, pk1, shifts[2]))):
        for variant, sh in enumerate((shift, 0)):
            dist = pq[:, None] - (pk[None, :] - sh)
            ok = (dist >= 0) & (dist <= ATT_BLOCK)
            bias = np.where(ok, 0.0, NEG).astype(np.float32)
            out[g, variant] = np.concatenate([bias, bias], axis=0)
    return out


def _attn_unit(q, kk, vv, bias, half0):
    zero = jnp.zeros_like(q)
    q2 = jnp.concatenate([jnp.where(half0, q, zero), jnp.where(half0, zero, q)], axis=0)
    s = lax.dot_general(q2, kk, (((1,), (1,)), ((), ())), preferred_element_type=F32) + bias
    m = jnp.max(s, axis=-1, keepdims=True)
    p = jnp.exp2(s - m).astype(BF16)
    vext = jnp.concatenate([vv, jnp.ones_like(vv)], axis=1)
    res = jnp.dot(p, vext, preferred_element_type=F32)
    top, bot = res[:ATT_BLOCK], res[ATT_BLOCK:]
    o = jnp.where(half0, top[:, :LANES], bot[:, :LANES])
    den = jnp.where(half0, top[:, LANES:], bot[:, LANES:])
    mb = jnp.where(half0, m[:ATT_BLOCK], m[ATT_BLOCK:])
    return o, mb, den


def _attention_kernel(q_ref, k_ref, v_ref, bias_ref, nw_ref, o_ref, qf_ref, kf_ref, vf_ref, st_ref):
    jt = pl.program_id(2)
    half0 = lax.broadcasted_iota(jnp.int32, (ATT_BLOCK, LANES), 1) < ATT_HEAD_DIM
    row0 = pl.multiple_of(jt * ATT_BLOCK, ATT_BLOCK)

    def lanes(r):
        return slice(LANES * r, LANES * (r + 1))

    @pl.when(jt == 0)
    def _():
        qf_ref[...] = q_ref[...].astype(F32)
        kf_ref[...] = k_ref[...].astype(F32)
        vf_ref[...] = v_ref[...].astype(F32)

    def store_stats(g, r, rows, stats):
        for st, val in enumerate(stats):
            st_ref[g, st, r, rows, :] = val

    k16 = pl.multiple_of(jnp.maximum(row0 - ATT_BLOCK, 0), ATT_BLOCK)
    bias16 = bias_ref[0, jnp.where(jt == 0, 1, 0)]
    units16 = []
    for r in range(MAX_DIL):
        def load(r=r):
            return (q_ref[pl.ds(row0, ATT_BLOCK), lanes(r)], k_ref[pl.ds(k16, 2 * ATT_BLOCK), lanes(r)],
                    v_ref[pl.ds(k16, 2 * ATT_BLOCK), lanes(r)], bias16)

        def store(stats, r=r):
            store_stats(0, r, slice(None), stats)
        units16.append((load, store))

    units4 = []
    for bb in range(4):
        gb = jt * 4 + bb
        for r0 in range(4):
            def load(gb=gb, r0=r0):
                q0 = pl.multiple_of(gb * 32, 32)
                k0 = pl.multiple_of(jnp.maximum(gb * 32 - 32, 0), 32)
                cols = [lanes(4 * r1 + r0) for r1 in range(4)]
                return (jnp.concatenate([q_ref[pl.ds(q0, 32), c] for c in cols], axis=0),
                        jnp.concatenate([k_ref[pl.ds(k0, 64), c] for c in cols], axis=0),
                        jnp.concatenate([v_ref[pl.ds(k0, 64), c] for c in cols], axis=0),
                        bias_ref[1, jnp.where(gb == 0, 1, 0)])

            def store(stats, bb=bb, r0=r0):
                for r1 in range(4):
                    store_stats(1, 4 * r1 + r0, pl.ds(32 * bb, 32), [x[32 * r1:32 * (r1 + 1)] for x in stats])
            units4.append((load, store))

    def gather(ref, start):
        return jnp.concatenate([ref[pl.ds(start, 8), lanes(r)] for r in range(MAX_DIL)], axis=0)

    units1 = []
    for bl in range(MAX_DIL):
        b = jt * MAX_DIL + bl

        def load(b=b):
            q0 = pl.multiple_of(b * 8, 8)
            k0 = pl.multiple_of(jnp.maximum(b * 8 - 8, 0), 8)
            return (gather(qf_ref, q0).astype(BF16),
                    jnp.concatenate([gather(kf_ref, k0), gather(kf_ref, k0 + 8)], axis=0).astype(BF16),
                    jnp.concatenate([gather(vf_ref, k0), gather(vf_ref, k0 + 8)], axis=0).astype(BF16),
                    bias_ref[2, jnp.where(b == 0, 1, 0)])

        def store(stats, bl=bl):
            for r in range(MAX_DIL):
                store_stats(2, r, pl.ds(8 * bl, 8), [x[8 * r:8 * (r + 1)] for x in stats])
        units1.append((load, store))

    for trio in zip(units16, units4, units1):
        for load, store in trio:
            store(_attn_unit(*load(), half0))

    nw = nw_ref[...]
    for r in range(MAX_DIL):
        ms = [st_ref[g, 1, r] for g in range(3)]
        mx = jnp.maximum(jnp.maximum(ms[0], ms[1]), ms[2])
        fs = [jnp.exp2(m - mx) for m in ms]
        num = fs[0] * st_ref[0, 0, r] + fs[1] * st_ref[1, 0, r] + fs[2] * st_ref[2, 0, r]
        den = fs[0] * st_ref[0, 2, r] + fs[1] * st_ref[1, 2, r] + fs[2] * st_ref[2, 2, r]
        att = num / den
        a2 = att * att
        s0 = jnp.sum(jnp.where(half0, a2, 0.0), axis=-1, keepdims=True)
        s1 = jnp.sum(jnp.where(half0, 0.0, a2), axis=-1, keepdims=True)
        msq = jnp.where(half0, s0, s1) * (1.0 / ATT_HEAD_DIM)
        o_ref[pl.ds(row0, ATT_BLOCK), lanes(r)] = (att * lax.rsqrt(msq + EPS) * nw).astype(BF16)


def _dilated_attention(aq, ak, av, norm_w, batch, seq_len):
    assert seq_len % ATT_TILE == 0 and seq_len >= 2 * ATT_TILE
    rows = seq_len // MAX_DIL
    nt = seq_len // ATT_TILE
    view = lambda t: t.reshape(ATT_PAIRS, batch, rows, MAX_DIL * LANES)
    bias = jnp.asarray(_attention_biases())
    seq_spec = pl.BlockSpec((None, None, rows, MAX_DIL * LANES), lambda b, p, j: (p, b, 0, 0))
    out = pl.pallas_call(
        _attention_kernel,
        grid=(batch, ATT_PAIRS, nt),
        in_specs=[seq_spec, seq_spec, seq_spec,
                  pl.BlockSpec(bias.shape, lambda b, p, j: (0, 0, 0, 0)),
                  pl.BlockSpec((None, 1, LANES), lambda b, p, j: (p, 0, 0))],
        out_specs=seq_spec,
        out_shape=jax.ShapeDtypeStruct((ATT_PAIRS, batch, rows, MAX_DIL * LANES), BF16),
        scratch_shapes=[pltpu.VMEM((rows, MAX_DIL * LANES), F32)] * 3
        + [pltpu.VMEM((3, 3, MAX_DIL, ATT_BLOCK, LANES), F32)],
        compiler_params=pltpu.CompilerParams(
            dimension_semantics=("arbitrary", "arbitrary", "arbitrary"), vmem_limit_bytes=VMEM_LIMIT),
        name="dilated_attention",
    )(view(aq), view(ak), view(av), bias, norm_w.reshape(ATT_PAIRS, 1, LANES))
    return out.reshape(ATT_PAIRS, batch * rows, MAX_DIL * LANES)


def _scan_lanes(x, op, fill):
    n = x.shape[1]
    lane = lax.broadcasted_iota(jnp.int32, x.shape, 1)
    sh = 1
    while sh < n:
        x = op(x, jnp.where(lane >= sh, pltpu.roll(x, sh, axis=1), fill))
        sh *= 2
    return x


def _two_term_rows(row):
    r = lax.broadcasted_iota(jnp.int32, (2 * SUBLANES, row.shape[1]), 0)
    full = jnp.broadcast_to(row, r.shape)
    hi = full.astype(BF16).astype(F32)
    return jnp.where(r == 0, hi, jnp.where(r == 1, full - hi, 0.0)).astype(BF16)


def _mlstm_kernel(q_ref, k_ref, v_ref, o_ref, g_ref, nw_ref, h_ref, c_ref, n_ref, m_ref, sc_ref, et_ref):
    Lc, D, H = MLSTM_KERNEL_CHUNK, MLSTM_HEAD_DIM, MLSTM_HEADS
    nb, ts = q_ref.shape[1], q_ref.shape[3]

    @pl.when(pl.program_id(1) == 0)
    def _():
        c_ref[...] = jnp.zeros(c_ref.shape, F32)
        n_ref[...] = jnp.zeros(n_ref.shape, F32)
        m_ref[...] = jnp.zeros(m_ref.shape, F32)

    si = lax.broadcasted_iota(jnp.int32, (Lc, Lc), 0)
    ti = lax.broadcasted_iota(jnp.int32, (Lc, Lc), 1)
    causal = si <= ti
    nc = ts // Lc

    tiles = [(c, b) for c in range(nc) for b in range(nb)]
    log_i = jnp.concatenate([g_ref[b, 0, :, Lc * c:Lc * (c + 1)] for c, b in tiles], axis=0)
    log_f = jnp.concatenate([g_ref[b, 1, :, Lc * c:Lc * (c + 1)] for c, b in tiles], axis=0)
    bcum = _scan_lanes(log_f, jnp.add, 0.0)
    e = log_i - bcum
    sc_ref[0] = bcum
    sc_ref[1] = e
    sc_ref[2] = _scan_lanes(e, jnp.maximum, NEG)
    rows_c = nb * SUBLANES
    for c in range(nc):
        e_c = e[rows_c * c:rows_c * (c + 1)]
        et_ref[c] = jnp.transpose(jnp.concatenate([e_c, jnp.zeros((Lc - rows_c, Lc), F32)], axis=0))

    def chunk(c, carry):
        t0 = pl.multiple_of(c * Lc, Lc)
        e_cols = et_ref[c]
        heads = [(b, h) for b in range(nb) for h in range(H)]
        rows = {}
        for b in range(nb):
            r0 = pl.multiple_of((c * nb + b) * SUBLANES, SUBLANES)
            bcum = sc_ref[0, pl.ds(r0, SUBLANES), :]
            e = sc_ref[1, pl.ds(r0, SUBLANES), :]
            m_prev = m_ref[b]
            g = bcum[:, Lc - 1:Lc]
            m_t = bcum + jnp.maximum(m_prev, sc_ref[2, pl.ds(r0, SUBLANES), :])
            c_minus_m = bcum - m_t
            a = g + e
            m_new = jnp.maximum(g + m_prev, jnp.max(a, axis=1, keepdims=True))
            rows[b] = dict(c_minus_m=c_minus_m, inter=jnp.exp(c_minus_m + m_prev), floor=jnp.exp(-m_t),
                           decay=jnp.exp(g + m_prev - m_new), w=jnp.exp(a - m_new))
            m_ref[b] = jnp.broadcast_to(m_new, m_prev.shape)

        qt = {bh: q_ref[bh[1], bh[0], :, pl.ds(t0, Lc)] for bh in heads}
        k = {bh: k_ref[bh[1], bh[0], pl.ds(t0, Lc), :] for bh in heads}
        vt = {bh: v_ref[bh[1], bh[0], :, pl.ds(t0, Lc)] for bh in heads}
        st = {bh: jnp.dot(k[bh], qt[bh], preferred_element_type=F32) for bh in heads}
        cq = {(b, h): jnp.dot(c_ref[b * H + h].astype(BF16), qt[b, h], preferred_element_type=F32) for b, h in heads}
        nq = {(b, h): jnp.dot(_two_term_rows(n_ref[b * H + h, 0:1, :]), qt[b, h], preferred_element_type=F32)
              for b, h in heads}
        scores = {}
        for b, h in heads:
            col = SUBLANES * b + h
            dprime = jnp.where(causal, e_cols[:, col:col + 1] + rows[b]["c_minus_m"][h:h + 1, :], NEG)
            scores[b, h] = st[b, h] * jnp.exp(dprime)
        num = {(b, h): jnp.dot(vt[b, h], scores[b, h].astype(BF16), preferred_element_type=F32)
               + rows[b]["inter"][h:h + 1, :] * cq[b, h] for b, h in heads}
        for b, h in heads:
            inter = rows[b]["inter"][h:h + 1, :]
            den = jnp.sum(scores[b, h], axis=0, keepdims=True) + inter * (nq[b, h][0:1, :] + nq[b, h][1:2, :])
            hh = num[b, h] * (1.0 / jnp.maximum(jnp.abs(den), rows[b]["floor"][h:h + 1, :]))
            hh = hh * lax.rsqrt(jnp.mean(hh * hh, axis=0, keepdims=True) + EPS) * nw_ref[h]
            gate = jax.nn.sigmoid(o_ref[h, b, :, pl.ds(t0, Lc)].astype(F32))
            h_ref[h, b, :, pl.ds(t0, Lc)] = (gate * hh).astype(BF16)
        for b, h in heads:
            s = b * H + h
            decay = rows[b]["decay"][h:h + 1, :]
            w_row = rows[b]["w"][h:h + 1, :]
            dc = jnp.dot((vt[b, h].astype(F32) * w_row).astype(BF16), k[b, h], preferred_element_type=F32)
            c_ref[s] = decay * c_ref[s] + dc
            dn = jnp.dot(_two_term_rows(w_row), k[b, h], preferred_element_type=F32)
            n_ref[s, 0:1, :] = decay * n_ref[s, 0:1, :] + dn[0:1, :] + dn[1:2, :]
        return carry

    lax.fori_loop(0, ts // Lc, chunk, 0, unroll=True)


def _mlstm(mqt, mk, mvt, mot, gates, norm_w, batch, seq_len):
    H, D = MLSTM_HEADS, MLSTM_HEAD_DIM
    ts = MLSTM_TILE
    nb = MLSTM_BATCHES if batch % MLSTM_BATCHES == 0 else 1
    assert seq_len % ts == 0 and ts % MLSTM_KERNEL_CHUNK == 0 and D == MLSTM_KERNEL_CHUNK
    nt = seq_len // ts
    tspec = pl.BlockSpec((H, nb, D, ts), lambda b, j: (0, b, 0, j))
    nw_cols = jnp.broadcast_to(norm_w.reshape(H, D, 1), (H, D, LANES))
    return pl.pallas_call(
        _mlstm_kernel,
        grid=(batch // nb, nt),
        in_specs=[tspec,
                  pl.BlockSpec((H, nb, ts, D), lambda b, j: (0, b, j, 0)),
                  tspec, tspec,
                  pl.BlockSpec((nb, 2, SUBLANES, ts), lambda b, j: (b, 0, 0, j)),
                  pl.BlockSpec((H, D, LANES), lambda b, j: (0, 0, 0))],
        out_specs=tspec,
        out_shape=jax.ShapeDtypeStruct((H, batch, D, seq_len), BF16),
        scratch_shapes=[pltpu.VMEM((nb * H, D, D), F32), pltpu.VMEM((nb * H, HALO, D), F32),
                        pltpu.VMEM((nb, SUBLANES, LANES), F32),
                        pltpu.VMEM((3, (ts // D) * nb * SUBLANES, D), F32), pltpu.VMEM((ts // D, D, D), F32)],
        compiler_params=pltpu.CompilerParams(
            dimension_semantics=("arbitrary", "arbitrary"), vmem_limit_bytes=VMEM_LIMIT),
        name="mlstm",
    )(mqt, mk.reshape(H, batch, seq_len, D), mvt, mot, gates, nw_cols)


def _ffn_kernel(x_ref, att_ref, hm_ref, permt_ref, wout_ref, fnw_ref, wup_ref, cw_ref, cb_ref, wdown_ref, finw_ref,
                o_ref, us_ref, *, tiles_per_seq, final_norm):
    tm = x_ref.shape[0]
    fc = FFN_CHUNK
    d_ff = wdown_ref.shape[0]
    nf = d_ff // fc
    i = pl.program_id(0)

    vrows = PERM_TILE // MAX_DIL
    att = jnp.concatenate(
        [jnp.dot(permt_ref[...],
                 jnp.concatenate(
                     [jnp.concatenate([att_ref[p, vrows * g:vrows * (g + 1), LANES * r:LANES * (r + 1)]
                                       for r in range(MAX_DIL)], axis=0) for p in range(ATT_PAIRS)], axis=1),
                 preferred_element_type=F32).astype(BF16)
         for g in range(tm // PERM_TILE)], axis=0)
    hm = [jnp.transpose(hm_ref[h].astype(F32)).astype(BF16) for h in range(MLSTM_HEADS)]
    y = jnp.concatenate([att] + hm, axis=1)
    h1 = x_ref[...] + jnp.dot(y, wout_ref[...], preferred_element_type=F32)
    o_ref[...] = h1
    xn = _rms(h1, fnw_ref[...]).astype(BF16)

    @pl.when(i % tiles_per_seq == 0)
    def _():
        us_ref[0:HALO, :] = jnp.zeros((HALO, us_ref.shape[1]), F32)

    us_ref[HALO:HALO + tm, :] = jnp.dot(xn, wup_ref[...], preferred_element_type=F32)

    def conv(c0):
        u = cb_ref[:, c0:c0 + fc]
        for j in range(FFN_CONV):
            off = HALO - (FFN_CONV - 1) + j
            u = u + us_ref[off:off + tm, c0:c0 + fc] * cw_ref[j:j + 1, c0:c0 + fc]
        return u

    acts = []
    for c in range(nf):
        gate, val = conv(c * fc), conv(d_ff + c * fc)
        acts.append((gate / (1.0 + jnp.exp(-gate)) * val).astype(BF16))
    us_ref[0:HALO, :] = us_ref[tm:tm + HALO, :]
    o_ref[...] += jnp.dot(jnp.concatenate(acts, axis=1), wdown_ref[...], preferred_element_type=F32)
    if final_norm:
        o_ref[...] = _rms(o_ref[...], finw_ref[...])


def _out_ffn(x2d, att, hm, w_out, ffn_norm_w, w_up, conv_w, conv_b, w_down, final_w, seq_len, final_norm):
    T, D = x2d.shape
    tm = FFN_TILE
    assert T % tm == 0 and seq_len % tm == 0
    d_ff = w_down.shape[0]
    assert d_ff % FFN_CHUNK == 0
    tps = seq_len // tm
    row = lambda i: (i, 0)
    const2 = lambda i: (0, 0)
    blk3 = lambda i: (0, i, 0)
    once = dict(pipeline_mode=pl.Buffered(1))
    assert tm % PERM_TILE == 0
    permt = jnp.asarray(_dilation_permutation(PERM_TILE).T, BF16)
    return pl.pallas_call(
        functools.partial(_ffn_kernel, tiles_per_seq=seq_len // tm, final_norm=final_norm),
        grid=(T // tm,),
        in_specs=[
            pl.BlockSpec((tm, D), row),
            pl.BlockSpec((ATT_PAIRS, tm // MAX_DIL, MAX_DIL * LANES), blk3),
            pl.BlockSpec((MLSTM_HEADS, None, MLSTM_HEAD_DIM, tm), lambda i: (0, i // tps, 0, i % tps)),
            pl.BlockSpec(permt.shape, const2),
            pl.BlockSpec(w_out.shape, const2, **once),
            pl.BlockSpec((1, D), const2),
            pl.BlockSpec(w_up.shape, const2, **once),
            pl.BlockSpec(conv_w.shape, const2),
            pl.BlockSpec(conv_b.shape, const2),
            pl.BlockSpec(w_down.shape, const2, **once),
            pl.BlockSpec((1, D), const2),
        ],
        out_specs=pl.BlockSpec((tm, D), row),
        out_shape=jax.ShapeDtypeStruct((T, D), F32),
        scratch_shapes=[pltpu.VMEM((HALO + tm, 2 * d_ff), F32)],
        compiler_params=pltpu.CompilerParams(
            dimension_semantics=("arbitrary",), vmem_limit_bytes=VMEM_LIMIT),
        name="out_proj_conv_ffn",
    )(x2d, att, hm, permt, w_out, ffn_norm_w, w_up, conv_w, conv_b, w_down, final_w)


def kernel(x, w_in, mlstm_conv_w, mlstm_conv_b, mlstm_i_bias, mlstm_f_bias, att_out_norm_w, mlstm_out_norm_w, w_out, mixer_norm_w, ffn_norm_w, w_ffn_up, ffn_conv_w, ffn_conv_b, w_ffn_down, final_norm_w):
    batch, seq_len, d_model = x.shape
    depth = w_in.shape[0]
    h = x.reshape(batch * seq_len, d_model)
    for layer in range(depth):
        w = w_in[layer]
        n_gate = 3 * ATT_WIDTH + 4 * MLSTM_WIDTH
        H = MLSTM_HEADS
        gate_cols = jnp.zeros((d_model, LANES), w.dtype)
        gate_cols = gate_cols.at[:, 0:H].set(w[:, n_gate:n_gate + H]).at[:, SUBLANES:SUBLANES + H].set(w[:, n_gate + H:])
        gate_b = jnp.zeros((1, LANES), F32)
        gate_b = gate_b.at[0, 0:H].set(mlstm_i_bias[layer]).at[0, SUBLANES:SUBLANES + H].set(mlstm_f_bias[layer])
        aq, ak, av, mq, mk, mv, mo, gates = _input_projection(
            h, mixer_norm_w[layer][None, :], w[:, :n_gate].astype(BF16), gate_cols.astype(BF16), mlstm_conv_w[layer],
            mlstm_conv_b[layer][None, :], gate_b, batch, seq_len)
        att = _dilated_attention(aq, ak, av, att_out_norm_w[layer], batch, seq_len)
        hm = _mlstm(mq, mk, mv, mo, gates, mlstm_out_norm_w[layer], batch, seq_len)
        h = _out_ffn(h, att, hm, w_out[layer].astype(BF16), ffn_norm_w[layer][None, :], w_ffn_up[layer].astype(BF16),
                     ffn_conv_w[layer], ffn_conv_b[layer][None, :], w_ffn_down[layer].astype(BF16),
                     final_norm_w[None, :], seq_len, final_norm=(layer == depth - 1))
    return h.reshape(batch, seq_len, d_model)
```
